```python
import math
import jax
import jax.numpy as jnp
from jax import lax
import numpy as np

D_MODEL = 1024
BATCH = 4
SEQ = 8192
DEPTH = 1
DEC_BATCH = 16
DEC_SEQ = 4096
PAST_LEN = 128

MEM_TOKENS = 256
MIX_WIDTH = D_MODEL
DIFF_WIDTH = MIX_WIDTH // 2
SSM_WIDTH = MIX_WIDTH - DIFF_WIDTH
DIFF_HEAD_DIM = 64
DIFF_HEADS = DIFF_WIDTH // (2 * DIFF_HEAD_DIM)
SSM_GROUP = 16
SSM_GROUPS = SSM_WIDTH // SSM_GROUP
SSM_STATE = 64
N_BUCKETS = 32
MAX_DISTANCE = 128
Q_BLOCK = 128
MEM_HEADS = 4
MEM_HEAD_DIM = D_MODEL // MEM_HEADS
D_FF = 2816
IN_WIDTH = 3 * DIFF_WIDTH + SSM_WIDTH
ALPHA = (2 * DEPTH) ** 0.25
BETA = (8 * DEPTH) ** -0.25
LN_EPS = 1e-5
SUBLN_EPS = 1e-5

kernel_name = 'hybrid_diffattn_s5_macaron_encoder'


def layer_norm(x, g, b):
    xf = x.astype(jnp.float32)
    mu = jnp.mean(xf, axis=-1, keepdims=True)
    var = jnp.mean(jnp.square(xf - mu), axis=-1, keepdims=True)
    y = (xf - mu) * lax.rsqrt(var + LN_EPS) * g.astype(jnp.float32) + b.astype(jnp.float32)
    return y.astype(x.dtype)


def swiglu(x, w13, w2):
    a, b = jnp.split(x @ w13, 2, axis=-1)
    return (jax.nn.silu(a) * b) @ w2


def rel_bucket(rel):
    half = N_BUCKETS // 2
    max_exact = half // 2
    ret = jnp.where(rel > 0, half, 0).astype(jnp.int32)
    n = jnp.abs(rel)
    nf = jnp.maximum(n, 1).astype(jnp.float32)
    large = max_exact + (jnp.log(nf / max_exact) / math.log(MAX_DISTANCE / max_exact)
                         * (half - max_exact)).astype(jnp.int32)
    large = jnp.minimum(large, half - 1)
    return ret + jnp.where(n < max_exact, n, large)


def diff_attention(q, k, v, lam, lam_init, subln_g, rel_bias):
    bsz, seq = q.shape[0], q.shape[1]
    nblk = seq // Q_BLOCK
    scale = DIFF_HEAD_DIM ** -0.5
    qb = q.reshape(bsz, nblk, Q_BLOCK, DIFF_HEADS, 2, DIFF_HEAD_DIM).transpose(1, 0, 2, 3, 4, 5)
    starts = jnp.arange(nblk, dtype=jnp.int32) * Q_BLOCK
    kpos = jnp.arange(seq, dtype=jnp.int32)
    table = rel_bias.astype(jnp.float32)

    def block(args):
        q_blk, s0 = args
        qpos = s0 + jnp.arange(Q_BLOCK, dtype=jnp.int32)
        bucket = rel_bucket(kpos[None, :] - qpos[:, None])
        bias = jnp.take(table, bucket, axis=0).transpose(2, 0, 1)
        s = jnp.einsum('bqhmd,bkhmd->bhmqk', q_blk, k).astype(jnp.float32) * scale
        p = jax.nn.softmax(s + bias[None, :, None], axis=-1)
        w = p[:, :, 0] - lam * p[:, :, 1]
        return jnp.einsum('bhqk,bkhe->bqhe', w.astype(v.dtype), v)

    o = lax.map(block, (qb, starts))
    o = o.transpose(1, 0, 2, 3, 4).reshape(bsz, seq, DIFF_HEADS, 2 * DIFF_HEAD_DIM)
    of = o.astype(jnp.float32)
    of = of * lax.rsqrt(jnp.mean(jnp.square(of), axis=-1, keepdims=True) + SUBLN_EPS)
    of = of * subln_g.astype(jnp.float32) * (1.0 - lam_init)
    return of.reshape(bsz, seq, DIFF_WIDTH).astype(q.dtype)


def _ssm_combine(c1, c2):
    a1, b1 = c1
    a2, b2 = c2
    return a1 * a2, a2 * b1 + b2


def s5_scan(u, lam_re, lam_im, log_step, b_re, b_im, c_re, c_im, reverse):
    f32 = jnp.float32
    lam = lax.complex(lam_re.astype(f32), lam_im.astype(f32))
    step = jnp.exp(log_step.astype(f32))[:, None]
    a_bar = jnp.exp(lam * step)
    b = lax.complex(b_re.astype(f32), b_im.astype(f32))
    b_bar = ((a_bar - 1.0) / lam)[..., None] * b
    bu = jnp.einsum('blgh,gph->blgp', u, b_bar)
    a = jnp.broadcast_to(a_bar, bu.shape)
    _, states = lax.associative_scan(_ssm_combine, (a, bu), reverse=reverse, axis=1)
    c = lax.complex(c_re.astype(f32), c_im.astype(f32))
    return jnp.einsum('blgp,ghp->blgh', states, c).real


def s5_mixer(u, lam_re, lam_im, log_step, b_re, b_im, c_re, c_im, d, glu_w, glu_b):
    bsz, seq = u.shape[0], u.shape[1]
    uf = u.astype(jnp.float32)
    ug = uf.reshape(bsz, seq, SSM_GROUPS, SSM_GROUP)
    y_fwd = s5_scan(ug, lam_re[0], lam_im[0], log_step[0], b_re[0], b_im[0], c_re[0], c_im[0], False)
    y_bwd = s5_scan(ug, lam_re[1], lam_im[1], log_step[1], b_re[1], b_im[1], c_re[1], c_im[1], True)
    y = (y_fwd + y_bwd).reshape(bsz, seq, SSM_WIDTH) + d.astype(jnp.float32) * uf
    z = jax.nn.gelu(y).astype(u.dtype)
    return z * jax.nn.sigmoid(z @ glu_w + glu_b)


def memory_attention(x, mem, wq, wkv, wo):
    bsz, seq = x.shape[0], x.shape[1]
    m = mem.shape[1]
    q = (x @ wq).reshape(bsz, seq, MEM_HEADS, MEM_HEAD_DIM)
    k, v = jnp.split(mem @ wkv, 2, axis=-1)
    k = k.reshape(bsz, m, MEM_HEADS, MEM_HEAD_DIM)
    v = v.reshape(bsz, m, MEM_HEADS, MEM_HEAD_DIM)
    s = jnp.einsum('bqhd,bkhd->bhqk', q, k).astype(jnp.float32) * (MEM_HEAD_DIM ** -0.5)
    p = jax.nn.softmax(s, axis=-1)
    o = jnp.einsum('bhqk,bkhd->bqhd', p.astype(v.dtype), v).reshape(bsz, seq, D_MODEL)
    return o @ wo


def encoder_trunk(x, mem, p):
    bsz, seq = x.shape[0], x.shape[1]
    for l in range(DEPTH):
        lam_init = 0.8 - 0.6 * math.exp(-0.3 * l)
        x = layer_norm(ALPHA * x + 0.5 * swiglu(x, p['ffn1_w13'][l], p['ffn1_w2'][l]),
                       p['ln_ffn1_g'][l], p['ln_ffn1_b'][l])
        h = x @ p['w_in'][l]
        q = h[..., :DIFF_WIDTH].reshape(bsz, seq, DIFF_HEADS, 2, DIFF_HEAD_DIM)
        k = h[..., DIFF_WIDTH:2 * DIFF_WIDTH].reshape(bsz, seq, DIFF_HEADS, 2, DIFF_HEAD_DIM)
        v = h[..., 2 * DIFF_WIDTH:3 * DIFF_WIDTH].reshape(bsz, seq, DIFF_HEADS, 2 * DIFF_HEAD_DIM)
        u = h[..., 3 * DIFF_WIDTH:]
        dl = p['diff_lambda'][l].astype(jnp.float32)
        lam = jnp.exp(jnp.sum(dl[0] * dl[1])) - jnp.exp(jnp.sum(dl[2] * dl[3])) + lam_init
        o_diff = diff_attention(q, k, v, lam, lam_init, p['diff_subln_g'][l], p['rel_bias'])
        o_ssm = s5_mixer(u, p['ssm_lam_re'][l], p['ssm_lam_im'][l], p['ssm_log_step'][l],
                         p['ssm_b_re'][l], p['ssm_b_im'][l], p['ssm_c_re'][l], p['ssm_c_im'][l],
                         p['ssm_d'][l], p['ssm_glu_w'][l], p['ssm_glu_b'][l])
        mix = jnp.concatenate([o_diff, o_ssm], axis=-1) @ p['w_out'][l]
        x = layer_norm(ALPHA * x + mix, p['ln_mix_g'][l], p['ln_mix_b'][l])
        x = layer_norm(ALPHA * x + memory_attention(x, mem, p['mem_wq'][l], p['mem_wkv'][l], p['mem_wo'][l]),
                       p['ln_mem_g'][l], p['ln_mem_b'][l])
        x = layer_norm(ALPHA * x + 0.5 * swiglu(x, p['ffn2_w13'][l], p['ffn2_w2'][l]),
                       p['ln_ffn2_g'][l], p['ln_ffn2_b'][l])
    return x


def setup_inputs(seed: int = 0) -> dict:
    key = jax.random.key(seed)
    keys = iter(jax.random.split(key, 64))
    f32 = jnp.float32

    def nrm(shape, scale):
        return jax.random.normal(next(keys), shape, f32) * scale

    def gain():
        return 1.0 + nrm((DEPTH, D_MODEL), 0.02)

    def bias():
        return nrm((DEPTH, D_MODEL), 0.02)

    d_in = D_MODEL ** -0.5
    in_col_scale = jnp.concatenate([jnp.ones((2 * DIFF_WIDTH,), f32),
                                    jnp.full((DIFF_WIDTH,), BETA, f32),
                                    jnp.ones((SSM_WIDTH,), f32)])
    kv_col_scale = jnp.concatenate([jnp.ones((D_MODEL,), f32), jnp.full((D_MODEL,), BETA, f32)])
    state_idx = jnp.arange(SSM_STATE, dtype=f32)
    ssm_shape = (DEPTH, 2, SSM_GROUPS, SSM_STATE)
    log_step = jax.random.uniform(next(keys), (DEPTH, 2, SSM_GROUPS), f32,
                                  math.log(1e-3), math.log(1e-1))
    return {
        'x_prompt': nrm((BATCH, SEQ, D_MODEL), 1.0),
        'x_sample': nrm((DEC_BATCH, DEC_SEQ, D_MODEL), 1.0),
        'mem_prompt': nrm((BATCH, MEM_TOKENS, D_MODEL), 1.0),
        'mem_sample': nrm((DEC_BATCH, MEM_TOKENS, D_MODEL), 1.0),
        'ffn1_w13': nrm((DEPTH, D_MODEL, 2 * D_FF), d_in * BETA),
        'ffn1_w2': nrm((DEPTH, D_FF, D_MODEL), D_FF ** -0.5 * BETA),
        'ln_ffn1_g': gain(),
        'ln_ffn1_b': bias(),
        'w_in': nrm((DEPTH, D_MODEL, IN_WIDTH), d_in) * in_col_scale,
        'diff_lambda': nrm((DEPTH, 4, DIFF_HEAD_DIM), 0.1),
        'diff_subln_g': 1.0 + nrm((DEPTH, 2 * DIFF_HEAD_DIM), 0.02),
        'rel_bias': nrm((N_BUCKETS, DIFF_HEADS), 0.1),
        'ssm_lam_re': -0.5 + nrm(ssm_shape, 0.01),
        'ssm_lam_im': math.pi * state_idx + nrm(ssm_shape, 0.01),
        'ssm_log_step': log_step,
        'ssm_b_re': nrm((DEPTH, 2, SSM_GROUPS, SSM_STATE, SSM_GROUP), (2 * SSM_GROUP) ** -0.5),
        'ssm_b_im': nrm((DEPTH, 2, SSM_GROUPS, SSM_STATE, SSM_GROUP), (2 * SSM_GROUP) ** -0.5),
        'ssm_c_re': nrm((DEPTH, 2, SSM_GROUPS, SSM_GROUP, SSM_STATE), 2.0 * (2 * SSM_STATE) ** -0.5),
        'ssm_c_im': nrm((DEPTH, 2, SSM_GROUPS, SSM_GROUP, SSM_STATE), 2.0 * (2 * SSM_STATE) ** -0.5),
        'ssm_d': nrm((DEPTH, SSM_WIDTH), 1.0),
        'ssm_glu_w': nrm((DEPTH, SSM_WIDTH, SSM_WIDTH), SSM_WIDTH ** -0.5),
        'ssm_glu_b': nrm((DEPTH, SSM_WIDTH), 0.01),
        'w_out': nrm((DEPTH, MIX_WIDTH, D_MODEL), MIX_WIDTH ** -0.5 * BETA),
        'ln_mix_g': gain(),
        'ln_mix_b': bias(),
        'mem_wq': nrm((DEPTH, D_MODEL, D_MODEL), d_in),
        'mem_wkv': nrm((DEPTH, D_MODEL, 2 * D_MODEL), d_in) * kv_col_scale,
        'mem_wo': nrm((DEPTH, D_MODEL, D_MODEL), d_in * BETA),
        'ln_mem_g': gain(),
        'ln_mem_b': bias(),
        'ffn2_w13': nrm((DEPTH, D_MODEL, 2 * D_FF), d_in * BETA),
        'ffn2_w2': nrm((DEPTH, D_FF, D_MODEL), D_FF ** -0.5 * BETA),
        'ln_ffn2_g': gain(),
        'ln_ffn2_b': bias(),
    }


def reference(x_prompt, x_sample, mem_prompt, mem_sample,
              ffn1_w13, ffn1_w2, ln_ffn1_g, ln_ffn1_b,
              w_in, diff_lambda, diff_subln_g, rel_bias,
              ssm_lam_re, ssm_lam_im, ssm_log_step, ssm_b_re, ssm_b_im, ssm_c_re, ssm_c_im,
              ssm_d, ssm_glu_w, ssm_glu_b,
              w_out, ln_mix_g, ln_mix_b,
              mem_wq, mem_wkv, mem_wo, ln_mem_g, ln_mem_b,
              ffn2_w13, ffn2_w2, ln_ffn2_g, ln_ffn2_b):
    params = dict(
        ffn1_w13=ffn1_w13, ffn1_w2=ffn1_w2, ln_ffn1_g=ln_ffn1_g, ln_ffn1_b=ln_ffn1_b,
        w_in=w_in, diff_lambda=diff_lambda, diff_subln_g=diff_subln_g, rel_bias=rel_bias,
        ssm_lam_re=ssm_lam_re, ssm_lam_im=ssm_lam_im, ssm_log_step=ssm_log_step,
        ssm_b_re=ssm_b_re, ssm_b_im=ssm_b_im, ssm_c_re=ssm_c_re, ssm_c_im=ssm_c_im,
        ssm_d=ssm_d, ssm_glu_w=ssm_glu_w, ssm_glu_b=ssm_glu_b,
        w_out=w_out, ln_mix_g=ln_mix_g, ln_mix_b=ln_mix_b,
        mem_wq=mem_wq, mem_wkv=mem_wkv, mem_wo=mem_wo, ln_mem_g=ln_mem_g, ln_mem_b=ln_mem_b,
        ffn2_w13=ffn2_w13, ffn2_w2=ffn2_w2, ln_ffn2_g=ln_ffn2_g, ln_ffn2_b=ln_ffn2_b,
    )
    y_prompt = encoder_trunk(x_prompt, mem_prompt, params)
    y_sample = encoder_trunk(x_sample, mem_sample, params)
    return (y_prompt, y_sample)
```

```python
import functools
import math

import jax
import jax.numpy as jnp
from jax import lax
from jax.experimental import pallas as pl
from jax.experimental.pallas import tpu as pltpu

F32 = jnp.float32
BF16 = jnp.bfloat16

DEPTH = 1
ALPHA = (2 * DEPTH) ** 0.25
LN_EPS = 1e-5
SUBLN_EPS = 1e-5
LAM_INIT = 0.8 - 0.6 * math.exp(-0.3 * 0)

LANES = 128
DIFF_HEADS = 4
DIFF_HEAD_DIM = 64
N_BUCKETS = 32
MEM_HEADS = 4
SSM_GROUP = 16
SSM_STATE = 64
SSM_CHUNK = 16
GROUPS_PER_SLAB = LANES // SSM_GROUP
SLAB_K = SSM_CHUNK * LANES
SLAB_STATE = GROUPS_PER_SLAB * SSM_STATE

VMEM_LIMIT_BYTES = 56 * 1024 * 1024


def _params(semantics):
    return pltpu.CompilerParams(dimension_semantics=semantics,
                                vmem_limit_bytes=VMEM_LIMIT_BYTES)


def _const_spec(shape):
    nd = len(shape)
    return pl.BlockSpec(shape, lambda *_: (0,) * nd, pipeline_mode=pl.Buffered(1))


def _layer_norm(y, g, b):
    mu = jnp.mean(y, axis=-1, keepdims=True)
    yc = y - mu
    var = jnp.mean(yc * yc, axis=-1, keepdims=True)
    return yc * lax.rsqrt(var + LN_EPS) * g + b


def _row_tile(n_rows, want):
    t = min(want, n_rows)
    assert n_rows % t == 0
    return t


def _ffn_body(x_ref, w1_ref, w3_ref, w2_ref, g_ref, b_ref, *rest, f_chunk, with_proj):
    x = x_ref[...]
    xb = x.astype(BF16)
    d_ff = w1_ref.shape[1]
    acc = jnp.zeros(x.shape, F32)
    for c in range(d_ff // f_chunk):
        sl = slice(c * f_chunk, (c + 1) * f_chunk)
        a = jnp.dot(xb, w1_ref[:, sl], preferred_element_type=F32)
        b = jnp.dot(xb, w3_ref[:, sl], preferred_element_type=F32)
        gated = (a * jax.nn.sigmoid(a) * b).astype(BF16)
        acc = acc + jnp.dot(gated, w2_ref[sl, :], preferred_element_type=F32)
    y = _layer_norm(ALPHA * x + 0.5 * acc, g_ref[...], b_ref[...])
    if not with_proj:
        (o_ref,) = rest
        o_ref[...] = y
        return
    win_ref, o_ref, q_ref, k_ref, v_ref, u_ref = rest
    o_ref[...] = y
    h = jnp.dot(y.astype(BF16), win_ref[...], preferred_element_type=F32)
    w = q_ref.shape[1]
    q_ref[...] = (h[:, :w] * (DIFF_HEAD_DIM ** -0.5)).astype(BF16)
    k_ref[...] = h[:, w:2 * w].astype(BF16)
    v_ref[...] = h[:, 2 * w:3 * w].astype(BF16)
    u_ref[...] = h[:, 3 * w:]


def _ffn_ln(x, w1, w3, w2, g, b, w_in=None, tm=512):
    n, d = x.shape
    d_ff = w1.shape[1]
    tm = _row_tile(n, tm)
    f_chunk = d_ff // 2 if (d_ff // 2) % LANES == 0 else d_ff
    row = lambda width: pl.BlockSpec((tm, width), lambda i: (i, 0))
    in_specs = [row(d), _const_spec(w1.shape), _const_spec(w3.shape), _const_spec(w2.shape),
                _const_spec(g.shape), _const_spec(b.shape)]
    args = [x, w1, w3, w2, g, b]
    out_shape = [jax.ShapeDtypeStruct((n, d), F32)]
    out_specs = [row(d)]
    if w_in is not None:
        w = w_in.shape[1] // 4
        in_specs.append(_const_spec(w_in.shape))
        args.append(w_in)
        out_shape += [jax.ShapeDtypeStruct((n, w), BF16)] * 3 + [jax.ShapeDtypeStruct((n, w), F32)]
        out_specs += [row(w)] * 4
    out = pl.pallas_call(
        functools.partial(_ffn_body, f_chunk=f_chunk, with_proj=w_in is not None),
        grid=(n // tm,),
        in_specs=in_specs, out_specs=out_specs, out_shape=out_shape,
        compiler_params=_params(("parallel",)),
        name="ffn_ln_proj" if w_in is not None else "ffn_ln",
    )(*args)
    return out if w_in is not None else out[0]


BIAS_TILE_OFFSETS = 5


def _bias_body(tab_ref, o_ref, *, tile):
    d = pl.program_id(0) - BIAS_TILE_OFFSETS // 2
    h = pl.program_id(1)
    row = lax.broadcasted_iota(jnp.int32, (tile, tile), 0)
    col = lax.broadcasted_iota(jnp.int32, (tile, tile), 1)
    rel = d * tile + col - row
    half = N_BUCKETS // 2
    max_exact = half // 2
    n = jnp.minimum(jnp.abs(rel), 128)
    n2 = n * n
    large = jnp.full((tile, tile), max_exact, jnp.int32)
    for t in range(1, half - max_exact):
        large = large + jnp.where(n2 >= (max_exact * max_exact) * 2 ** t, 1, 0)
    bucket = jnp.where(rel > 0, half, 0) + jnp.where(n < max_exact, n, large)
    val = jnp.full((tile, tile), tab_ref[0, h], F32)
    for bkt in range(1, N_BUCKETS):
        val = jnp.where(bucket == bkt, tab_ref[bkt, h], val)
    o_ref[0, 0] = val


def _bias_tiles(rel_bias, tile):
    assert tile >= 128
    n_heads = rel_bias.shape[1]
    return pl.pallas_call(
        functools.partial(_bias_body, tile=tile),
        grid=(BIAS_TILE_OFFSETS, n_heads),
        in_specs=[pl.BlockSpec(memory_space=pltpu.SMEM)],
        out_specs=pl.BlockSpec((1, 1, tile, tile), lambda d, h: (d, h, 0, 0)),
        out_shape=jax.ShapeDtypeStruct((BIAS_TILE_OFFSETS, n_heads, tile, tile), F32),
        compiler_params=_params(("parallel", "parallel")),
        name="rel_bias_tiles",
    )(rel_bias.astype(F32))


def _diff_attn_body(dl_ref, g_ref, q_ref, k_ref, v_ref, bias_ref, o_ref,
                    qm_scr, m_scr, l_scr, acc_scr):
    ki = pl.program_id(3)
    tk = k_ref.shape[1]

    @pl.when(ki == 0)
    def _init():
        q = q_ref[0]
        lane = lax.broadcasted_iota(jnp.int32, q.shape, 1)
        zero = jnp.zeros_like(q)
        qm_scr[0] = jnp.where(lane < DIFF_HEAD_DIM, q, zero)
        qm_scr[1] = jnp.where(lane >= DIFF_HEAD_DIM, q, zero)
        m_scr[...] = jnp.full(m_scr.shape, -jnp.inf, F32)
        l_scr[...] = jnp.zeros(l_scr.shape, F32)
        acc_scr[...] = jnp.zeros(acc_scr.shape, F32)

    k = k_ref[0]
    v = v_ref[0]
    bias = bias_ref[0, 0]
    for m in range(2):
        s = lax.dot_general(qm_scr[m], k, (((1,), (1,)), ((), ())),
                            preferred_element_type=F32) + bias
        m_prev = m_scr[m]
        m_next = jnp.maximum(m_prev, jnp.max(s, axis=1, keepdims=True))
        alpha = jnp.exp(m_prev - m_next)
        p = jnp.exp(s - pltpu.repeat(m_next, tk // LANES, axis=1))
        l_scr[m] = alpha * l_scr[m] + jnp.sum(p, axis=1, keepdims=True)
        acc_scr[m] = alpha * acc_scr[m] + jnp.dot(p.astype(BF16), v, preferred_element_type=F32)
        m_scr[m] = m_next

    @pl.when(ki == pl.num_programs(3) - 1)
    def _finish():
        dl = dl_ref[...]
        lam = (jnp.exp(jnp.sum(dl[0:1] * dl[1:2], axis=1, keepdims=True))
               - jnp.exp(jnp.sum(dl[2:3] * dl[3:4], axis=1, keepdims=True)) + LAM_INIT)
        o = acc_scr[0] / l_scr[0] - lam * (acc_scr[1] / l_scr[1])
        o = o * lax.rsqrt(jnp.mean(o * o, axis=-1, keepdims=True) + SUBLN_EPS)
        o_ref[0] = (o * g_ref[...] * (1.0 - LAM_INIT)).astype(o_ref.dtype)


def _diff_attention(q, k, v, bias_tiles, diff_lambda, subln_g, tile):
    bsz, seq, width = q.shape
    n_heads = width // LANES
    assert seq % tile == 0 and tile % LANES == 0
    nt = seq // tile
    half = BIAS_TILE_OFFSETS // 2

    def bias_index(b, h, qi, ki):
        return (jnp.clip(ki - qi, -half, half) + half, h, 0, 0)

    return pl.pallas_call(
        _diff_attn_body,
        grid=(bsz, n_heads, nt, nt),
        in_specs=[
            pl.BlockSpec(diff_lambda.shape, lambda b, h, qi, ki: (0, 0)),
            pl.BlockSpec(subln_g.shape, lambda b, h, qi, ki: (0, 0)),
            pl.BlockSpec((1, tile, LANES), lambda b, h, qi, ki: (b, qi, h)),
            pl.BlockSpec((1, tile, LANES), lambda b, h, qi, ki: (b, ki, h)),
            pl.BlockSpec((1, tile, LANES), lambda b, h, qi, ki: (b, ki, h)),
            pl.BlockSpec((1, 1, tile, tile), bias_index),
        ],
        out_specs=pl.BlockSpec((1, tile, LANES), lambda b, h, qi, ki: (b, qi, h)),
        out_shape=jax.ShapeDtypeStruct((bsz, seq, width), BF16),
        scratch_shapes=[
            pltpu.VMEM((2, tile, LANES), BF16),
            pltpu.VMEM((2, tile, LANES), F32),
            pltpu.VMEM((2, tile, LANES), F32),
            pltpu.VMEM((2, tile, LANES), F32),
        ],
        compiler_params=_params(("parallel", "parallel", "parallel", "arbitrary")),
        name="diff_attention",
    )(diff_lambda, subln_g, q, k, v, bias_tiles)


def _ssm_matrices(lam_re, lam_im, log_step, b_re, b_im, c_re, c_im):
    t_len = SSM_CHUNK
    n_groups = lam_re.shape[1]
    n_slabs = n_groups // GROUPS_PER_SLAB
    lam = lax.complex(lam_re.astype(F32), lam_im.astype(F32))
    step = jnp.exp(log_step.astype(F32))[..., None]
    lam_dt = lam * step
    a_bar = jnp.exp(lam_dt)
    b_bar = ((a_bar - 1.0) / lam)[..., None] * lax.complex(b_re.astype(F32), b_im.astype(F32))
    c = lax.complex(c_re.astype(F32), c_im.astype(F32))
    ramp = jnp.arange(t_len + 1, dtype=F32)[None, :, None, None]
    pows = jnp.exp(lam_dt[:, None] * ramp)
    rpows = jnp.exp(lam_dt[:, None] * (t_len - ramp))

    kern = jnp.einsum('dghp,dngp,dgpi->dnghi', c, pows[:, :t_len], b_bar).real
    s_idx = jnp.arange(t_len)[:, None]
    t_idx = jnp.arange(t_len)[None, :]
    fwd = jnp.where((t_idx >= s_idx)[..., None, None, None],
                    kern[0][jnp.clip(t_idx - s_idx, 0, t_len - 1)], 0.0)
    bwd = jnp.where((s_idx >= t_idx)[..., None, None, None],
                    kern[1][jnp.clip(s_idx - t_idx, 0, t_len - 1)], 0.0)
    m_full = (fwd + bwd).transpose(2, 0, 4, 1, 3)
    eye = jnp.eye(GROUPS_PER_SLAB, dtype=F32)
    m_full = m_full.reshape(n_slabs, GROUPS_PER_SLAB, t_len, SSM_GROUP, t_len, SSM_GROUP)
    m_slab = jnp.einsum('oqshtk,qr->osqhtrk', m_full, eye).reshape(n_slabs, SLAB_K, SLAB_K)

    f_fwd = rpows[0, 1:t_len + 1][:, :, :, None] * b_bar[0][None]
    f_bwd = pows[1, :t_len][:, :, :, None] * b_bar[1][None]
    f_c = jnp.stack([f_fwd, f_bwd])
    f_ri = jnp.stack([f_c.real, f_c.imag], axis=1)
    f_ri = f_ri.reshape(2, 2, t_len, n_slabs, GROUPS_PER_SLAB, SSM_STATE, SSM_GROUP)
    f_slab = jnp.einsum('drjoqpi,qs->ojqidrsp', f_ri, eye).reshape(n_slabs, SLAB_K, 4 * SLAB_STATE)

    e_fwd = c[0][None] * pows[0, 1:t_len + 1][:, :, None, :]
    e_bwd = c[1][None] * rpows[1, :t_len][:, :, None, :]
    e_c = jnp.stack([e_fwd, e_bwd])
    e_ri = jnp.stack([e_c.real, -e_c.imag], axis=1)
    e_ri = e_ri.reshape(2, 2, t_len, n_slabs, GROUPS_PER_SLAB, SSM_GROUP, SSM_STATE)
    e_slab = jnp.einsum('drioqhp,qs->odrqpish', e_ri, eye).reshape(n_slabs, 4 * SLAB_STATE, SLAB_K)

    w_out = jnp.concatenate([m_slab, e_slab], axis=1).astype(BF16)
    a_chunk = pows[:, t_len].reshape(2, n_slabs, SLAB_STATE).transpose(1, 0, 2)
    return f_slab.astype(BF16), w_out, a_chunk.real, a_chunk.imag


def _load_chunks(u_ref, a_scr, nct):
    for s in range(SSM_CHUNK):
        a_scr[:, s * LANES:(s + 1) * LANES] = (
            u_ref[0, pl.ds(s, nct, stride=SSM_CHUNK), :].astype(BF16))


def _ssm_summary_body(u_ref, f_ref, s_ref, a_scr):
    nct = s_ref.shape[1]
    _load_chunks(u_ref, a_scr, nct)
    s_ref[0] = jnp.dot(a_scr[...], f_ref[0], preferred_element_type=F32)


def _ssm_scan_body(ar_ref, ai_ref, s_ref, x_ref):
    nc = s_ref.shape[1]
    w = SLAB_STATE
    ar_f, ai_f = ar_ref[0, 0:1, :], ai_ref[0, 0:1, :]
    ar_b, ai_b = ar_ref[0, 1:2, :], ai_ref[0, 1:2, :]

    def step(c, carry):
        fr, fi, br, bi = carry
        cb = nc - 1 - c
        x_ref[0, pl.ds(c, 1), 0:w] = fr
        x_ref[0, pl.ds(c, 1), w:2 * w] = fi
        x_ref[0, pl.ds(cb, 1), 2 * w:3 * w] = br
        x_ref[0, pl.ds(cb, 1), 3 * w:4 * w] = bi
        sfr = s_ref[0, pl.ds(c, 1), 0:w]
        sfi = s_ref[0, pl.ds(c, 1), w:2 * w]
        sbr = s_ref[0, pl.ds(cb, 1), 2 * w:3 * w]
        sbi = s_ref[0, pl.ds(cb, 1), 3 * w:4 * w]
        return (ar_f * fr - ai_f * fi + sfr, ar_f * fi + ai_f * fr + sfi,
                ar_b * br - ai_b * bi + sbr, ar_b * bi + ai_b * br + sbi)

    zero = jnp.zeros((1, w), F32)
    lax.fori_loop(0, nc, step, (zero, zero, zero, zero))


def _ssm_output_body(u_ref, x_ref, w_ref, y_ref, a_scr):
    nct = x_ref.shape[1]
    _load_chunks(u_ref, a_scr, nct)
    a_scr[:, SLAB_K:] = x_ref[0].astype(BF16)
    y = jnp.dot(a_scr[...], w_ref[0], preferred_element_type=F32)
    for t in range(SSM_CHUNK):
        y_ref[0, pl.ds(t, nct, stride=SSM_CHUNK), :] = y[:, t * LANES:(t + 1) * LANES]


def _s5_scan(u, f_slab, w_out, a_re, a_im, chunks_per_tile=256):
    bsz, seq, width = u.shape
    n_slabs = width // LANES
    nc = seq // SSM_CHUNK
    nct = _row_tile(nc, chunks_per_tile)
    rows = nct * SSM_CHUNK
    state_w = 4 * SLAB_STATE
    grid = (n_slabs, bsz, nc // nct)
    u_spec = pl.BlockSpec((1, rows, LANES), lambda o, b, i: (b, i, o))
    st_spec = pl.BlockSpec((1, nct, state_w), lambda o, b, i: (b, i, o))

    sums = pl.pallas_call(
        _ssm_summary_body, grid=grid,
        in_specs=[u_spec, pl.BlockSpec((1, SLAB_K, state_w), lambda o, b, i: (o, 0, 0))],
        out_specs=st_spec,
        out_shape=jax.ShapeDtypeStruct((bsz, nc, n_slabs * state_w), F32),
        scratch_shapes=[pltpu.VMEM((nct, SLAB_K), BF16)],
        compiler_params=_params(("parallel", "parallel", "parallel")),
        name="s5_chunk_summaries",
    )(u, f_slab)

    carried = pl.pallas_call(
        _ssm_scan_body, grid=(n_slabs, bsz),
        in_specs=[pl.BlockSpec((1, 2, SLAB_STATE), lambda o, b: (o, 0, 0)),
                  pl.BlockSpec((1, 2, SLAB_STATE), lambda o, b: (o, 0, 0)),
                  pl.BlockSpec((1, nc, state_w), lambda o, b: (b, 0, o))],
        out_specs=pl.BlockSpec((1, nc, state_w), lambda o, b: (b, 0, o)),
        out_shape=jax.ShapeDtypeStruct(sums.shape, F32),
        compiler_params=_params(("parallel", "parallel")),
        name="s5_chunk_scan",
    )(a_re, a_im, sums)

    return pl.pallas_call(
        _ssm_output_body, grid=grid,
        in_specs=[u_spec, st_spec,
                  pl.BlockSpec((1, SLAB_K + state_w, SLAB_K), lambda o, b, i: (o, 0, 0),
                               pipeline_mode=pl.Buffered(1))],
        out_specs=u_spec,
        out_shape=jax.ShapeDtypeStruct((bsz, seq, width), F32),
        scratch_shapes=[pltpu.VMEM((nct, SLAB_K + state_w), BF16)],
        compiler_params=_params(("parallel", "parallel", "parallel")),
        name="s5_outputs",
    )(u, carried, w_out)


def _mix_body(x_ref, od_ref, y_ref, u_ref, d_ref, gw_ref, gb_ref, wo_ref, g_ref, b_ref, o_ref):
    x = x_ref[...]
    y = y_ref[...] + d_ref[...] * u_ref[...]
    z = jax.nn.gelu(y)
    gate = jax.nn.sigmoid(jnp.dot(z.astype(BF16), gw_ref[...], preferred_element_type=F32)
                          + gb_ref[...])
    o_ssm = (z * gate).astype(BF16)
    dw = od_ref.shape[1]
    mix = (jnp.dot(od_ref[...], wo_ref[:dw, :], preferred_element_type=F32)
           + jnp.dot(o_ssm, wo_ref[dw:, :], preferred_element_type=F32))
    o_ref[...] = _layer_norm(ALPHA * x + mix, g_ref[...], b_ref[...])


def _mix_ln(x, o_diff, y_ssm, u, d, glu_w, glu_b, w_out, g, b, tm=512):
    n, dm = x.shape
    w = u.shape[1]
    tm = _row_tile(n, tm)
    row = lambda width: pl.BlockSpec((tm, width), lambda i: (i, 0))
    consts = [d, glu_w, glu_b, w_out, g, b]
    return pl.pallas_call(
        _mix_body, grid=(n // tm,),
        in_specs=[row(dm), row(w), row(w), row(w)] + [_const_spec(c.shape) for c in consts],
        out_specs=row(dm),
        out_shape=jax.ShapeDtypeStruct((n, dm), F32),
        compiler_params=_params(("parallel",)),
        name="mix_ln",
    )(x, o_diff, y_ssm, u, *consts)


def _matmul_body(x_ref, w_ref, o_ref):
    o_ref[...] = jnp.dot(x_ref[...].astype(BF16), w_ref[...],
                         preferred_element_type=F32).astype(o_ref.dtype)


def _matmul_rows(x, w, out_dtype, tm=512):
    n, kd = x.shape
    tm = _row_tile(n, tm)
    return pl.pallas_call(
        _matmul_body, grid=(n // tm,),
        in_specs=[pl.BlockSpec((tm, kd), lambda i: (i, 0)), _const_spec(w.shape)],
        out_specs=pl.BlockSpec((tm, w.shape[1]), lambda i: (i, 0)),
        out_shape=jax.ShapeDtypeStruct((n, w.shape[1]), out_dtype),
        compiler_params=_params(("parallel",)),
        name="mem_kv_proj",
    )(x, w)


def _mem_attn_body(x_ref, kv_ref, wq_ref, wo_ref, g_ref, b_ref, o_ref):
    x = x_ref[...]
    dm = x.shape[1]
    hd = dm // MEM_HEADS
    q = (jnp.dot(x.astype(BF16), wq_ref[...], preferred_element_type=F32)
         * (hd ** -0.5)).astype(BF16)
    heads = []
    for h in range(MEM_HEADS):
        sl = slice(h * hd, (h + 1) * hd)
        s = lax.dot_general(q[:, sl], kv_ref[0, :, sl], (((1,), (1,)), ((), ())),
                            preferred_element_type=F32)
        e = jnp.exp(s - jnp.max(s, axis=1, keepdims=True))
        p = e / jnp.sum(e, axis=1, keepdims=True)
        heads.append(jnp.dot(p.astype(BF16), kv_ref[0, :, dm + h * hd:dm + (h + 1) * hd],
                             preferred_element_type=F32))
    o = jnp.concatenate(heads, axis=1).astype(BF16)
    att = jnp.dot(o, wo_ref[...], preferred_element_type=F32)
    o_ref[...] = _layer_norm(ALPHA * x + att, g_ref[...], b_ref[...])


def _mem_attn_ln(x, kv, seq, wq, wo, g, b, tm=512):
    n, dm = x.shape
    tm = _row_tile(seq, tm)
    tiles_per_seq = seq // tm
    consts = [wq, wo, g, b]
    return pl.pallas_call(
        _mem_attn_body, grid=(n // tm,),
        in_specs=[pl.BlockSpec((tm, dm), lambda i: (i, 0)),
                  pl.BlockSpec((1,) + kv.shape[1:], lambda i: (i // tiles_per_seq, 0, 0))]
                 + [_const_spec(c.shape) for c in consts],
        out_specs=pl.BlockSpec((tm, dm), lambda i: (i, 0)),
        out_shape=jax.ShapeDtypeStruct((n, dm), F32),
        compiler_params=_params(("parallel",)),
        name="mem_attn_ln",
    )(x, kv, *consts)


def _attn_tile(seq):
    for t in (512, 256, 128):
        if seq % t == 0:
            return t
    raise ValueError(f"sequence length {seq} is not a multiple of 128")


def _trunk(x, mem, p, bias_tiles, ssm_mats, attn_tile):
    bsz, seq, dm = x.shape
    n = bsz * seq
    f_slab, w_ssm, a_re, a_im = ssm_mats

    x1, q, k, v, u = _ffn_ln(x.reshape(n, dm), p['ffn1_w1'], p['ffn1_w3'], p['ffn1_w2'],
                             p['ln_ffn1_g'], p['ln_ffn1_b'], w_in=p['w_in'])
    w = q.shape[1]
    o_diff = _diff_attention(q.reshape(bsz, seq, w), k.reshape(bsz, seq, w),
                             v.reshape(bsz, seq, w), bias_tiles,
                             p['diff_lambda'], p['diff_subln_g'], attn_tile)
    y_ssm = _s5_scan(u.reshape(bsz, seq, w), f_slab, w_ssm, a_re, a_im)
    x2 = _mix_ln(x1, o_diff.reshape(n, w), y_ssm.reshape(n, w), u, p['ssm_d'], p['ssm_glu_w'],
                 p['ssm_glu_b'], p['w_out'], p['ln_mix_g'], p['ln_mix_b'])
    kv = _matmul_rows(mem.reshape(-1, dm), p['mem_wkv'], BF16).reshape(bsz, mem.shape[1], -1)
    x3 = _mem_attn_ln(x2, kv, seq, p['mem_wq'], p['mem_wo'], p['ln_mem_g'], p['ln_mem_b'])
    x4 = _ffn_ln(x3, p['ffn2_w1'], p['ffn2_w3'], p['ffn2_w2'], p['ln_ffn2_g'], p['ln_ffn2_b'])
    return x4.reshape(bsz, seq, dm)


def kernel(x_prompt, x_sample, mem_prompt, mem_sample, ffn1_w13, ffn1_w2, ln_ffn1_g, ln_ffn1_b, w_in, diff_lambda, diff_subln_g, rel_bias, ssm_lam_re, ssm_lam_im, ssm_log_step, ssm_b_re, ssm_b_im, ssm_c_re, ssm_c_im, ssm_d, ssm_glu_w, ssm_glu_b, w_out, ln_mix_g, ln_mix_b, mem_wq, mem_wkv, mem_wo, ln_mem_g, ln_mem_b, ffn2_w13, ffn2_w2, ln_ffn2_g, ln_ffn2_b):
    l = 0
    d_ff = ffn1_w2.shape[1]
    p = dict(
        ffn1_w1=ffn1_w13[l, :, :d_ff].astype(BF16), ffn1_w3=ffn1_w13[l, :, d_ff:].astype(BF16),
        ffn1_w2=ffn1_w2[l].astype(BF16), ln_ffn1_g=ln_ffn1_g.astype(F32), ln_ffn1_b=ln_ffn1_b.astype(F32),
        w_in=w_in[l].astype(BF16),
        diff_lambda=diff_lambda[l].astype(F32), diff_subln_g=diff_subln_g.astype(F32),
        ssm_d=ssm_d.astype(F32), ssm_glu_w=ssm_glu_w[l].astype(BF16), ssm_glu_b=ssm_glu_b.astype(F32),
        w_out=w_out[l].astype(BF16), ln_mix_g=ln_mix_g.astype(F32), ln_mix_b=ln_mix_b.astype(F32),
        mem_wq=mem_wq[l].astype(BF16), mem_wkv=mem_wkv[l].astype(BF16), mem_wo=mem_wo[l].astype(BF16),
        ln_mem_g=ln_mem_g.astype(F32), ln_mem_b=ln_mem_b.astype(F32),
        ffn2_w1=ffn2_w13[l, :, :d_ff].astype(BF16), ffn2_w3=ffn2_w13[l, :, d_ff:].astype(BF16),
        ffn2_w2=ffn2_w2[l].astype(BF16), ln_ffn2_g=ln_ffn2_g.astype(F32), ln_ffn2_b=ln_ffn2_b.astype(F32),
    )
    ssm_mats = _ssm_matrices(ssm_lam_re[l], ssm_lam_im[l], ssm_log_step[l], ssm_b_re[l], ssm_b_im[l],
                             ssm_c_re[l], ssm_c_im[l])
    outs = []
    tiles = {}
    for x, mem in ((x_prompt, mem_prompt), (x_sample, mem_sample)):
        t = _attn_tile(x.shape[1])
        if t not in tiles:
            tiles[t] = _bias_tiles(rel_bias, t)
        outs.append(_trunk(x, mem, p, tiles[t], ssm_mats, t))
    return tuple(outs)
```

```python
import functools
import math

import jax
import jax.numpy as jnp
from jax import lax
from jax.experimental import pallas as pl
from jax.experimental.pallas import tpu as pltpu

F32 = jnp.float32
BF16 = jnp.bfloat16

DEPTH = 1
ALPHA = (2 * DEPTH) ** 0.25
LN_EPS = 1e-5
SUBLN_EPS = 1e-5
LAM_INIT = 0.8 - 0.6 * math.exp(-0.3 * 0)
LOG2_E = 1.0 / math.log(2.0)

LANES = 128
MXU_DEPTH = 256
DIFF_HEADS = 4
DIFF_HEAD_DIM = 64
N_BUCKETS = 32
MEM_HEADS = 4
SSM_GROUP = 16
SSM_STATE = 64
SSM_CHUNK = 16
GROUPS_PER_SLAB = LANES // SSM_GROUP
SLAB_K = SSM_CHUNK * LANES
SLAB_STATE = GROUPS_PER_SLAB * SSM_STATE

VMEM_LIMIT_BYTES = 56 * 1024 * 1024


def _params(semantics, flags=None):
    return pltpu.CompilerParams(dimension_semantics=semantics,
                                vmem_limit_bytes=VMEM_LIMIT_BYTES, flags=flags)


def _const_spec(shape):
    nd = len(shape)
    return pl.BlockSpec(shape, lambda *_: (0,) * nd, pipeline_mode=pl.Buffered(1))


def _layer_norm(y, g, b):
    mu = jnp.mean(y, axis=-1, keepdims=True)
    yc = y - mu
    var = jnp.mean(yc * yc, axis=-1, keepdims=True)
    return yc * lax.rsqrt(var + LN_EPS) * g + b


def _row_tile(n_rows, want):
    t = min(want, n_rows)
    assert n_rows % t == 0
    return t


def _ffn_body(x_ref, w1_ref, w3_ref, w2_ref, g_ref, b_ref, *rest, f_chunk, with_proj):
    x = x_ref[...]
    xb = x.astype(BF16)
    d_ff = w1_ref.shape[1]
    acc = jnp.zeros(x.shape, F32)
    for c in range(d_ff // f_chunk):
        sl = slice(c * f_chunk, (c + 1) * f_chunk)
        a = jnp.dot(xb, w1_ref[:, sl], preferred_element_type=F32)
        b = jnp.dot(xb, w3_ref[:, sl], preferred_element_type=F32)
        gated = (a * jax.nn.sigmoid(a) * b).astype(BF16)
        acc = acc + jnp.dot(gated, w2_ref[sl, :], preferred_element_type=F32)
    y = _layer_norm(ALPHA * x + 0.5 * acc, g_ref[...], b_ref[...])
    if not with_proj:
        (o_ref,) = rest
        o_ref[...] = y
        return
    win_ref, o_ref, q_ref, k_ref, v_ref, u_ref = rest
    o_ref[...] = y
    h = jnp.dot(y.astype(BF16), win_ref[...], preferred_element_type=F32)
    w = q_ref.shape[1]
    q_ref[...] = (h[:, :w] * (DIFF_HEAD_DIM ** -0.5 * LOG2_E)).astype(BF16)
    k_ref[...] = h[:, w:2 * w].astype(BF16)
    v_ref[...] = h[:, 2 * w:3 * w].astype(BF16)
    u_ref[...] = h[:, 3 * w:]


def _ffn_ln(x, w1, w3, w2, g, b, w_in=None, tm=512):
    n, d = x.shape
    d_ff = w1.shape[1]
    tm = _row_tile(n, tm)
    f_chunk = d_ff // 2 if (d_ff // 2) % LANES == 0 else d_ff
    row = lambda width: pl.BlockSpec((tm, width), lambda i: (i, 0))
    in_specs = [row(d), _const_spec(w1.shape), _const_spec(w3.shape), _const_spec(w2.shape),
                _const_spec(g.shape), _const_spec(b.shape)]
    args = [x, w1, w3, w2, g, b]
    out_shape = [jax.ShapeDtypeStruct((n, d), F32)]
    out_specs = [row(d)]
    if w_in is not None:
        w = w_in.shape[1] // 4
        in_specs.append(_const_spec(w_in.shape))
        args.append(w_in)
        out_shape += [jax.ShapeDtypeStruct((n, w), BF16)] * 3 + [jax.ShapeDtypeStruct((n, w), F32)]
        out_specs += [row(w)] * 4
    out = pl.pallas_call(
        functools.partial(_ffn_body, f_chunk=f_chunk, with_proj=w_in is not None),
        grid=(n // tm,),
        in_specs=in_specs, out_specs=out_specs, out_shape=out_shape,
        compiler_params=_params(("parallel",)),
        name="ffn_ln_proj" if w_in is not None else "ffn_ln",
    )(*args)
    return out if w_in is not None else out[0]


BIAS_TILE_OFFSETS = 5


def _bias_body(tab_ref, o_ref, *, tile):
    d = pl.program_id(0) - BIAS_TILE_OFFSETS // 2
    h = pl.program_id(1)
    row = lax.broadcasted_iota(jnp.int32, (tile, tile), 0)
    col = lax.broadcasted_iota(jnp.int32, (tile, tile), 1)
    rel = d * tile + col - row
    half = N_BUCKETS // 2
    max_exact = half // 2
    n = jnp.minimum(jnp.abs(rel), 128)
    n2 = n * n
    large = jnp.full((tile, tile), max_exact, jnp.int32)
    for t in range(1, half - max_exact):
        large = large + jnp.where(n2 >= (max_exact * max_exact) * 2 ** t, 1, 0)
    bucket = jnp.where(rel > 0, half, 0) + jnp.where(n < max_exact, n, large)
    val = jnp.full((tile, tile), tab_ref[0, h], F32)
    for bkt in range(1, N_BUCKETS):
        val = jnp.where(bucket == bkt, tab_ref[bkt, h], val)
    o_ref[0, 0] = val * LOG2_E


def _bias_tiles(rel_bias, tile):
    assert tile >= 128
    n_heads = rel_bias.shape[1]
    return pl.pallas_call(
        functools.partial(_bias_body, tile=tile),
        grid=(BIAS_TILE_OFFSETS, n_heads),
        in_specs=[pl.BlockSpec(memory_space=pltpu.SMEM)],
        out_specs=pl.BlockSpec((1, 1, tile, tile), lambda d, h: (d, h, 0, 0)),
        out_shape=jax.ShapeDtypeStruct((BIAS_TILE_OFFSETS, n_heads, tile, tile), F32),
        compiler_params=_params(("parallel", "parallel")),
        name="rel_bias_tiles",
    )(rel_bias.astype(F32))


def _diff_attn_body(dl_ref, g_ref, q_ref, k_ref, v_ref, bias_ref, o_ref,
                    qm_scr, s_a, s_b, pm_a, pm_b, m_scr, l_scr, acc_scr, *, tile):
    qi = pl.program_id(2)
    nk = k_ref.shape[1] // tile
    half = BIAS_TILE_OFFSETS // 2

    q = q_ref[0]
    lane = lax.broadcasted_iota(jnp.int32, q.shape, 1)
    zero = jnp.zeros_like(q)
    qm_scr[0] = jnp.where(lane < DIFF_HEAD_DIM, q, zero)
    qm_scr[1] = jnp.where(lane >= DIFF_HEAD_DIM, q, zero)
    m_scr[...] = jnp.full(m_scr.shape, -jnp.inf, F32)
    l_scr[...] = jnp.zeros(l_scr.shape, F32)
    acc_scr[...] = jnp.zeros(acc_scr.shape, F32)

    def scores(j, buf):
        s_ref, pm_ref = buf
        k = k_ref[0, pl.ds(pl.multiple_of(j * tile, tile), tile), :]
        bias = bias_ref[jnp.clip(j - qi, -half, half) + half, 0]
        for m in range(2):
            s = lax.dot_general(qm_scr[m], k, (((1,), (1,)), ((), ())),
                                preferred_element_type=F32) + bias
            s_ref[m] = s
            pm = s[:, :LANES]
            for c in range(1, tile // LANES):
                pm = jnp.maximum(pm, s[:, c * LANES:(c + 1) * LANES])
            pm_ref[m] = pm

    def softmax_pv(j, buf):
        s_ref, pm_ref = buf
        v = v_ref[0, pl.ds(pl.multiple_of(j * tile, tile), tile), :]
        for m in range(2):
            m_prev = m_scr[m]
            m_next = jnp.maximum(m_prev, jnp.max(pm_ref[m], axis=1, keepdims=True))
            alpha = jnp.exp2(m_prev - m_next)
            acc = alpha * acc_scr[m]
            psum = jnp.zeros_like(alpha)
            m_rep = jnp.concatenate([m_next] * (MXU_DEPTH // LANES), axis=1)
            for c in range(tile // MXU_DEPTH):
                cols = slice(c * MXU_DEPTH, (c + 1) * MXU_DEPTH)
                p = jnp.exp2(s_ref[m, :, cols] - m_rep)
                for i in range(MXU_DEPTH // LANES):
                    psum = psum + p[:, i * LANES:(i + 1) * LANES]
                acc = acc + jnp.dot(p.astype(BF16), v[cols, :], preferred_element_type=F32)
            l_scr[m] = alpha * l_scr[m] + jnp.sum(psum, axis=1, keepdims=True)
            acc_scr[m] = acc
            m_scr[m] = m_next

    buf_a, buf_b = (s_a, pm_a), (s_b, pm_b)
    scores(0, buf_a)

    def pair(jj, carry):
        j = 2 * jj
        scores(j + 1, buf_b)
        softmax_pv(j, buf_a)
        scores(j + 2, buf_a)
        softmax_pv(j + 1, buf_b)
        return carry

    lax.fori_loop(0, nk // 2 - 1, pair, 0)
    scores(nk - 1, buf_b)
    softmax_pv(nk - 2, buf_a)
    softmax_pv(nk - 1, buf_b)

    dl = dl_ref[...]
    lam = (jnp.exp(jnp.sum(dl[0:1] * dl[1:2], axis=1, keepdims=True))
           - jnp.exp(jnp.sum(dl[2:3] * dl[3:4], axis=1, keepdims=True)) + LAM_INIT)
    o = acc_scr[0] / l_scr[0] - lam * (acc_scr[1] / l_scr[1])
    o = o * lax.rsqrt(jnp.mean(o * o, axis=-1, keepdims=True) + SUBLN_EPS)
    o_ref[0] = (o * g_ref[...] * (1.0 - LAM_INIT)).astype(o_ref.dtype)


def _diff_attention(q, k, v, bias_tiles, diff_lambda, subln_g, tile):
    bsz, seq, width = q.shape
    n_heads = width // LANES
    assert seq % (2 * tile) == 0 and tile % MXU_DEPTH == 0

    return pl.pallas_call(
        functools.partial(_diff_attn_body, tile=tile),
        grid=(bsz, n_heads, seq // tile),
        in_specs=[
            pl.BlockSpec(diff_lambda.shape, lambda b, h, qi: (0, 0)),
            pl.BlockSpec(subln_g.shape, lambda b, h, qi: (0, 0)),
            pl.BlockSpec((1, tile, LANES), lambda b, h, qi: (b, qi, h)),
            pl.BlockSpec((1, seq, LANES), lambda b, h, qi: (b, 0, h)),
            pl.BlockSpec((1, seq, LANES), lambda b, h, qi: (b, 0, h)),
            pl.BlockSpec((BIAS_TILE_OFFSETS, 1, tile, tile), lambda b, h, qi: (0, h, 0, 0)),
        ],
        out_specs=pl.BlockSpec((1, tile, LANES), lambda b, h, qi: (b, qi, h)),
        out_shape=jax.ShapeDtypeStruct((bsz, seq, width), BF16),
        scratch_shapes=[
            pltpu.VMEM((2, tile, LANES), BF16),
            pltpu.VMEM((2, tile, tile), F32),
            pltpu.VMEM((2, tile, tile), F32),
            pltpu.VMEM((2, tile, LANES), F32),
            pltpu.VMEM((2, tile, LANES), F32),
            pltpu.VMEM((2, tile, LANES), F32),
            pltpu.VMEM((2, tile, LANES), F32),
            pltpu.VMEM((2, tile, LANES), F32),
        ],
        compiler_params=_params(("parallel", "parallel", "arbitrary")),
        name="diff_attention",
    )(diff_lambda, subln_g, q, k, v, bias_tiles)


def _ssm_matrices(lam_re, lam_im, log_step, b_re, b_im, c_re, c_im):
    t_len = SSM_CHUNK
    n_groups = lam_re.shape[1]
    n_slabs = n_groups // GROUPS_PER_SLAB
    lam = lax.complex(lam_re.astype(F32), lam_im.astype(F32))
    step = jnp.exp(log_step.astype(F32))[..., None]
    lam_dt = lam * step
    a_bar = jnp.exp(lam_dt)
    b_bar = ((a_bar - 1.0) / lam)[..., None] * lax.complex(b_re.astype(F32), b_im.astype(F32))
    c = lax.complex(c_re.astype(F32), c_im.astype(F32))
    ramp = jnp.arange(t_len + 1, dtype=F32)[None, :, None, None]
    pows = jnp.exp(lam_dt[:, None] * ramp)
    rpows = jnp.exp(lam_dt[:, None] * (t_len - ramp))

    kern = jnp.einsum('dghp,dngp,dgpi->dnghi', c, pows[:, :t_len], b_bar).real
    s_idx = jnp.arange(t_len)[:, None]
    t_idx = jnp.arange(t_len)[None, :]
    fwd = jnp.where((t_idx >= s_idx)[..., None, None, None],
                    kern[0][jnp.clip(t_idx - s_idx, 0, t_len - 1)], 0.0)
    bwd = jnp.where((s_idx >= t_idx)[..., None, None, None],
                    kern[1][jnp.clip(s_idx - t_idx, 0, t_len - 1)], 0.0)
    m_full = (fwd + bwd).transpose(2, 0, 4, 1, 3)
    eye = jnp.eye(GROUPS_PER_SLAB, dtype=F32)
    m_full = m_full.reshape(n_slabs, GROUPS_PER_SLAB, t_len, SSM_GROUP, t_len, SSM_GROUP)
    m_slab = jnp.einsum('oqshtk,qr->osqhtrk', m_full, eye).reshape(n_slabs, SLAB_K, SLAB_K)

    f_fwd = rpows[0, 1:t_len + 1][:, :, :, None] * b_bar[0][None]
    f_bwd = pows[1, :t_len][:, :, :, None] * b_bar[1][None]
    f_c = jnp.stack([f_fwd, f_bwd])
    f_ri = jnp.stack([f_c.real, f_c.imag], axis=1)
    f_ri = f_ri.reshape(2, 2, t_len, n_slabs, GROUPS_PER_SLAB, SSM_STATE, SSM_GROUP)
    f_slab = jnp.einsum('drjoqpi,qs->ojqidrsp', f_ri, eye).reshape(n_slabs, SLAB_K, 4 * SLAB_STATE)

    e_fwd = c[0][None] * pows[0, 1:t_len + 1][:, :, None, :]
    e_bwd = c[1][None] * rpows[1, :t_len][:, :, None, :]
    e_c = jnp.stack([e_fwd, e_bwd])
    e_ri = jnp.stack([e_c.real, -e_c.imag], axis=1)
    e_ri = e_ri.reshape(2, 2, t_len, n_slabs, GROUPS_PER_SLAB, SSM_GROUP, SSM_STATE)
    e_slab = jnp.einsum('drioqhp,qs->odrqpish', e_ri, eye).reshape(n_slabs, 4 * SLAB_STATE, SLAB_K)

    w_out = jnp.concatenate([m_slab, e_slab], axis=1).astype(BF16)
    a_chunk = pows[:, t_len].reshape(2, n_slabs, SLAB_STATE).transpose(1, 0, 2)
    return f_slab.astype(BF16), w_out, a_chunk.real, a_chunk.imag


def _load_chunks(u_ref, a_scr, nct):
    for s in range(SSM_CHUNK):
        a_scr[:, s * LANES:(s + 1) * LANES] = (
            u_ref[0, pl.ds(s, nct, stride=SSM_CHUNK), :].astype(BF16))


def _ssm_summary_body(u_ref, f_ref, s_ref, a_scr):
    nct = s_ref.shape[1]
    _load_chunks(u_ref, a_scr, nct)
    s_ref[0] = jnp.dot(a_scr[...], f_ref[0], preferred_element_type=F32)


def _ssm_scan_body(ar_ref, ai_ref, s_ref, x_ref):
    nc = s_ref.shape[1]
    w = SLAB_STATE
    ar_f, ai_f = ar_ref[0, 0:1, :], ai_ref[0, 0:1, :]
    ar_b, ai_b = ar_ref[0, 1:2, :], ai_ref[0, 1:2, :]

    def step(c, carry):
        fr, fi, br, bi = carry
        cb = nc - 1 - c
        x_ref[0, pl.ds(c, 1), 0:w] = fr
        x_ref[0, pl.ds(c, 1), w:2 * w] = fi
        x_ref[0, pl.ds(cb, 1), 2 * w:3 * w] = br
        x_ref[0, pl.ds(cb, 1), 3 * w:4 * w] = bi
        sfr = s_ref[0, pl.ds(c, 1), 0:w]
        sfi = s_ref[0, pl.ds(c, 1), w:2 * w]
        sbr = s_ref[0, pl.ds(cb, 1), 2 * w:3 * w]
        sbi = s_ref[0, pl.ds(cb, 1), 3 * w:4 * w]
        return (ar_f * fr - ai_f * fi + sfr, ar_f * fi + ai_f * fr + sfi,
                ar_b * br - ai_b * bi + sbr, ar_b * bi + ai_b * br + sbi)

    zero = jnp.zeros((1, w), F32)
    lax.fori_loop(0, nc, step, (zero, zero, zero, zero))


def _ssm_output_body(u_ref, x_ref, w_ref, y_ref, a_scr):
    nct = x_ref.shape[1]
    _load_chunks(u_ref, a_scr, nct)
    a_scr[:, SLAB_K:] = x_ref[0].astype(BF16)
    y = jnp.dot(a_scr[...], w_ref[0], preferred_element_type=F32)
    for t in range(SSM_CHUNK):
        y_ref[0, pl.ds(t, nct, stride=SSM_CHUNK), :] = y[:, t * LANES:(t + 1) * LANES]


def _s5_scan(u, f_slab, w_out, a_re, a_im, chunks_per_tile=256):
    bsz, seq, width = u.shape
    n_slabs = width // LANES
    nc = seq // SSM_CHUNK
    nct = _row_tile(nc, chunks_per_tile)
    rows = nct * SSM_CHUNK
    state_w = 4 * SLAB_STATE
    grid = (n_slabs, bsz, nc // nct)
    u_spec = pl.BlockSpec((1, rows, LANES), lambda o, b, i: (b, i, o))
    st_spec = pl.BlockSpec((1, nct, state_w), lambda o, b, i: (b, i, o))

    sums = pl.pallas_call(
        _ssm_summary_body, grid=grid,
        in_specs=[u_spec, pl.BlockSpec((1, SLAB_K, state_w), lambda o, b, i: (o, 0, 0))],
        out_specs=st_spec,
        out_shape=jax.ShapeDtypeStruct((bsz, nc, n_slabs * state_w), F32),
        scratch_shapes=[pltpu.VMEM((nct, SLAB_K), BF16)],
        compiler_params=_params(("parallel", "parallel", "parallel")),
        name="s5_chunk_summaries",
    )(u, f_slab)

    carried = pl.pallas_call(
        _ssm_scan_body, grid=(n_slabs, bsz),
        in_specs=[pl.BlockSpec((1, 2, SLAB_STATE), lambda o, b: (o, 0, 0)),
                  pl.BlockSpec((1, 2, SLAB_STATE), lambda o, b: (o, 0, 0)),
                  pl.BlockSpec((1, nc, state_w), lambda o, b: (b, 0, o))],
        out_specs=pl.BlockSpec((1, nc, state_w), lambda o, b: (b, 0, o)),
        out_shape=jax.ShapeDtypeStruct(sums.shape, F32),
        compiler_params=_params(("parallel", "parallel")),
        name="s5_chunk_scan",
    )(a_re, a_im, sums)

    return pl.pallas_call(
        _ssm_output_body, grid=grid,
        in_specs=[u_spec, st_spec,
                  pl.BlockSpec((1, SLAB_K + state_w, SLAB_K), lambda o, b, i: (o, 0, 0),
                               pipeline_mode=pl.Buffered(1))],
        out_specs=u_spec,
        out_shape=jax.ShapeDtypeStruct((bsz, seq, width), F32),
        scratch_shapes=[pltpu.VMEM((nct, SLAB_K + state_w), BF16)],
        compiler_params=_params(("parallel", "parallel", "parallel")),
        name="s5_outputs",
    )(u, carried, w_out)


def _mix_body(x_ref, od_ref, y_ref, u_ref, d_ref, gw_ref, gb_ref, wo_ref, g_ref, b_ref, o_ref):
    x = x_ref[...]
    y = y_ref[...] + d_ref[...] * u_ref[...]
    z = jax.nn.gelu(y)
    gate = jax.nn.sigmoid(jnp.dot(z.astype(BF16), gw_ref[...], preferred_element_type=F32)
                          + gb_ref[...])
    o_ssm = (z * gate).astype(BF16)
    dw = od_ref.shape[1]
    mix = (jnp.dot(od_ref[...], wo_ref[:dw, :], preferred_element_type=F32)
           + jnp.dot(o_ssm, wo_ref[dw:, :], preferred_element_type=F32))
    o_ref[...] = _layer_norm(ALPHA * x + mix, g_ref[...], b_ref[...])


def _mix_ln(x, o_diff, y_ssm, u, d, glu_w, glu_b, w_out, g, b, tm=512):
    n, dm = x.shape
    w = u.shape[1]
    tm = _row_tile(n, tm)
    row = lambda width: pl.BlockSpec((tm, width), lambda i: (i, 0))
    consts = [d, glu_w, glu_b, w_out, g, b]
    return pl.pallas_call(
        _mix_body, grid=(n // tm,),
        in_specs=[row(dm), row(w), row(w), row(w)] + [_const_spec(c.shape) for c in consts],
        out_specs=row(dm),
        out_shape=jax.ShapeDtypeStruct((n, dm), F32),
        compiler_params=_params(("parallel",)),
        name="mix_ln",
    )(x, o_diff, y_ssm, u, *consts)


def _matmul_body(x_ref, w_ref, o_ref):
    o_ref[...] = jnp.dot(x_ref[...].astype(BF16), w_ref[...],
                         preferred_element_type=F32).astype(o_ref.dtype)


def _matmul_rows(x, w, out_dtype, tm=512):
    n, kd = x.shape
    tm = _row_tile(n, tm)
    return pl.pallas_call(
        _matmul_body, grid=(n // tm,),
        in_specs=[pl.BlockSpec((tm, kd), lambda i: (i, 0)), _const_spec(w.shape)],
        out_specs=pl.BlockSpec((tm, w.shape[1]), lambda i: (i, 0)),
        out_shape=jax.ShapeDtypeStruct((n, w.shape[1]), out_dtype),
        compiler_params=_params(("parallel",)),
        name="mem_kv_proj",
    )(x, w)


def _mem_attn_body(x_ref, kv_ref, wq_ref, wo_ref, g_ref, b_ref, o_ref):
    x = x_ref[...]
    dm = x.shape[1]
    hd = dm // MEM_HEADS
    q = (jnp.dot(x.astype(BF16), wq_ref[...], preferred_element_type=F32)
         * (hd ** -0.5)).astype(BF16)
    heads = []
    for h in range(MEM_HEADS):
        sl = slice(h * hd, (h + 1) * hd)
        s = lax.dot_general(q[:, sl], kv_ref[0, :, sl], (((1,), (1,)), ((), ())),
                            preferred_element_type=F32)
        e = jnp.exp(s - jnp.max(s, axis=1, keepdims=True))
        p = e / jnp.sum(e, axis=1, keepdims=True)
        heads.append(jnp.dot(p.astype(BF16), kv_ref[0, :, dm + h * hd:dm + (h + 1) * hd],
                             preferred_element_type=F32))
    o = jnp.concatenate(heads, axis=1).astype(BF16)
    att = jnp.dot(o, wo_ref[...], preferred_element_type=F32)
    o_ref[...] = _layer_norm(ALPHA * x + att, g_ref[...], b_ref[...])


def _mem_attn_ln(x, kv, seq, wq, wo, g, b, tm=512):
    n, dm = x.shape
    tm = _row_tile(seq, tm)
    tiles_per_seq = seq // tm
    consts = [wq, wo, g, b]
    return pl.pallas_call(
        _mem_attn_body, grid=(n // tm,),
        in_specs=[pl.BlockSpec((tm, dm), lambda i: (i, 0)),
                  pl.BlockSpec((1,) + kv.shape[1:], lambda i: (i // tiles_per_seq, 0, 0))]
                 + [_const_spec(c.shape) for c in consts],
        out_specs=pl.BlockSpec((tm, dm), lambda i: (i, 0)),
        out_shape=jax.ShapeDtypeStruct((n, dm), F32),
        compiler_params=_params(("parallel",)),
        name="mem_attn_ln",
    )(x, kv, *consts)


def _attn_tile(seq):
    for t in (512, 256):
        if seq % (2 * t) == 0:
            return t
    raise ValueError(f"sequence length {seq} is not a multiple of 512")


def _trunk(x, mem, p, bias_tiles, ssm_mats, attn_tile):
    bsz, seq, dm = x.shape
    n = bsz * seq
    f_slab, w_ssm, a_re, a_im = ssm_mats

    x1, q, k, v, u = _ffn_ln(x.reshape(n, dm), p['ffn1_w1'], p['ffn1_w3'], p['ffn1_w2'],
                             p['ln_ffn1_g'], p['ln_ffn1_b'], w_in=p['w_in'])
    w = q.shape[1]
    o_diff = _diff_attention(q.reshape(bsz, seq, w), k.reshape(bsz, seq, w),
                             v.reshape(bsz, seq, w), bias_tiles,
                             p['diff_lambda'], p['diff_subln_g'], attn_tile)
    y_ssm = _s5_scan(u.reshape(bsz, seq, w), f_slab, w_ssm, a_re, a_im)
    x2 = _mix_ln(x1, o_diff.reshape(n, w), y_ssm.reshape(n, w), u, p['ssm_d'], p['ssm_glu_w'],
                 p['ssm_glu_b'], p['w_out'], p['ln_mix_g'], p['ln_mix_b'])
    kv = _matmul_rows(mem.reshape(-1, dm), p['mem_wkv'], BF16).reshape(bsz, mem.shape[1], -1)
    x3 = _mem_attn_ln(x2, kv, seq, p['mem_wq'], p['mem_wo'], p['ln_mem_g'], p['ln_mem_b'])
    x4 = _ffn_ln(x3, p['ffn2_w1'], p['ffn2_w3'], p['ffn2_w2'], p['ln_ffn2_g'], p['ln_ffn2_b'])
    return x4.reshape(bsz, seq, dm)


def kernel(x_prompt, x_sample, mem_prompt, mem_sample, ffn1_w13, ffn1_w2, ln_ffn1_g, ln_ffn1_b, w_in, diff_lambda, diff_subln_g, rel_bias, ssm_lam_re, ssm_lam_im, ssm_log_step, ssm_b_re, ssm_b_im, ssm_c_re, ssm_c_im, ssm_d, ssm_glu_w, ssm_glu_b, w_out, ln_mix_g, ln_mix_b, mem_wq, mem_wkv, mem_wo, ln_mem_g, ln_mem_b, ffn2_w13, ffn2_w2, ln_ffn2_g, ln_ffn2_b):
    l = 0
    d_ff = ffn1_w2.shape[1]
    p = dict(
        ffn1_w1=ffn1_w13[l, :, :d_ff].astype(BF16), ffn1_w3=ffn1_w13[l, :, d_ff:].astype(BF16),
        ffn1_w2=ffn1_w2[l].astype(BF16), ln_ffn1_g=ln_ffn1_g.astype(F32), ln_ffn1_b=ln_ffn1_b.astype(F32),
        w_in=w_in[l].astype(BF16),
        diff_lambda=diff_lambda[l].astype(F32), diff_subln_g=diff_subln_g.astype(F32),
        ssm_d=ssm_d.astype(F32), ssm_glu_w=ssm_glu_w[l].astype(BF16), ssm_glu_b=ssm_glu_b.astype(F32),
        w_out=w_out[l].astype(BF16), ln_mix_g=ln_mix_g.astype(F32), ln_mix_b=ln_mix_b.astype(F32),
        mem_wq=mem_wq[l].astype(BF16), mem_wkv=mem_wkv[l].astype(BF16), mem_wo=mem_wo[l].astype(BF16),
        ln_mem_g=ln_mem_g.astype(F32), ln_mem_b=ln_mem_b.astype(F32),
        ffn2_w1=ffn2_w13[l, :, :d_ff].astype(BF16), ffn2_w3=ffn2_w13[l, :, d_ff:].astype(BF16),
        ffn2_w2=ffn2_w2[l].astype(BF16), ln_ffn2_g=ln_ffn2_g.astype(F32), ln_ffn2_b=ln_ffn2_b.astype(F32),
    )
    ssm_mats = _ssm_matrices(ssm_lam_re[l], ssm_lam_im[l], ssm_log_step[l], ssm_b_re[l], ssm_b_im[l],
                             ssm_c_re[l], ssm_c_im[l])
    outs = []
    tiles = {}
    for x, mem in ((x_prompt, mem_prompt), (x_sample, mem_sample)):
        t = _attn_tile(x.shape[1])
        if t not in tiles:
            tiles[t] = _bias_tiles(rel_bias, t)
        outs.append(_trunk(x, mem, p, tiles[t], ssm_mats, t))
    return tuple(outs)
```

```python
import functools
import math

import jax
import jax.numpy as jnp
from jax import lax
from jax.experimental import pallas as pl
from jax.experimental.pallas import tpu as pltpu

F32 = jnp.float32
BF16 = jnp.bfloat16

DEPTH = 1
ALPHA = (2 * DEPTH) ** 0.25
LN_EPS = 1e-5
SUBLN_EPS = 1e-5
LAM_INIT = 0.8 - 0.6 * math.exp(-0.3 * 0)
LOG2_E = 1.0 / math.log(2.0)

LANES = 128
MXU_DEPTH = 256
DIFF_HEADS = 4
DIFF_HEAD_DIM = 64
N_BUCKETS = 32
MEM_HEADS = 4
SSM_GROUP = 16
SSM_STATE = 64
SSM_CHUNK = 16
GROUPS_PER_SLAB = LANES // SSM_GROUP
SLAB_K = SSM_CHUNK * LANES
SLAB_STATE = GROUPS_PER_SLAB * SSM_STATE

VMEM_LIMIT_BYTES = 56 * 1024 * 1024


def _params(semantics, flags=None):
    return pltpu.CompilerParams(dimension_semantics=semantics,
                                vmem_limit_bytes=VMEM_LIMIT_BYTES, flags=flags)


def _const_spec(shape):
    nd = len(shape)
    return pl.BlockSpec(shape, lambda *_: (0,) * nd, pipeline_mode=pl.Buffered(1))


def _layer_norm(y, g, b):
    mu = jnp.mean(y, axis=-1, keepdims=True)
    yc = y - mu
    var = jnp.mean(yc * yc, axis=-1, keepdims=True)
    return yc * lax.rsqrt(var + LN_EPS) * g + b


def _row_tile(n_rows, want):
    t = min(want, n_rows)
    assert n_rows % t == 0
    return t


def _ffn_body(x_ref, w1_ref, w3_ref, w2_ref, g_ref, b_ref, *rest, f_chunk, with_proj):
    x = x_ref[...]
    xb = x.astype(BF16)
    d_ff = w1_ref.shape[1]
    acc = jnp.zeros(x.shape, F32)
    for c in range(d_ff // f_chunk):
        sl = slice(c * f_chunk, (c + 1) * f_chunk)
        a = jnp.dot(xb, w1_ref[:, sl], preferred_element_type=F32)
        b = jnp.dot(xb, w3_ref[:, sl], preferred_element_type=F32)
        gated = (a * jax.nn.sigmoid(a) * b).astype(BF16)
        acc = acc + jnp.dot(gated, w2_ref[sl, :], preferred_element_type=F32)
    y = _layer_norm(ALPHA * x + 0.5 * acc, g_ref[...], b_ref[...])
    if not with_proj:
        (o_ref,) = rest
        o_ref[...] = y
        return
    win_ref, o_ref, q_ref, k_ref, v_ref, u_ref = rest
    o_ref[...] = y
    h = jnp.dot(y.astype(BF16), win_ref[...], preferred_element_type=F32)
    w = q_ref.shape[1]
    q_ref[...] = (h[:, :w] * (DIFF_HEAD_DIM ** -0.5 * LOG2_E)).astype(BF16)
    k_ref[...] = h[:, w:2 * w].astype(BF16)
    v_ref[...] = h[:, 2 * w:3 * w].astype(BF16)
    u_ref[...] = h[:, 3 * w:]


def _ffn_ln(x, w1, w3, w2, g, b, w_in=None, tm=512):
    n, d = x.shape
    d_ff = w1.shape[1]
    tm = _row_tile(n, tm)
    f_chunk = d_ff // 2 if (d_ff // 2) % LANES == 0 else d_ff
    row = lambda width: pl.BlockSpec((tm, width), lambda i: (i, 0))
    in_specs = [row(d), _const_spec(w1.shape), _const_spec(w3.shape), _const_spec(w2.shape),
                _const_spec(g.shape), _const_spec(b.shape)]
    args = [x, w1, w3, w2, g, b]
    out_shape = [jax.ShapeDtypeStruct((n, d), F32)]
    out_specs = [row(d)]
    if w_in is not None:
        w = w_in.shape[1] // 4
        in_specs.append(_const_spec(w_in.shape))
        args.append(w_in)
        out_shape += [jax.ShapeDtypeStruct((n, w), BF16)] * 3 + [jax.ShapeDtypeStruct((n, w), F32)]
        out_specs += [row(w)] * 4
    out = pl.pallas_call(
        functools.partial(_ffn_body, f_chunk=f_chunk, with_proj=w_in is not None),
        grid=(n // tm,),
        in_specs=in_specs, out_specs=out_specs, out_shape=out_shape,
        compiler_params=_params(("parallel",)),
        name="ffn_ln_proj" if w_in is not None else "ffn_ln",
    )(*args)
    return out if w_in is not None else out[0]


MAX_DISTANCE = 128


def _bias_offset_range(tq, tk):
    d_lo = -((tk + MAX_DISTANCE - 1 + tq - 1) // tq)
    d_hi = 1 + (MAX_DISTANCE - 1 + tq - 1) // tq
    return d_lo, d_hi


def _bias_body(tab_ref, o_ref, *, tq, tk, d_lo):
    d = pl.program_id(0) + d_lo
    h = pl.program_id(1)
    row = lax.broadcasted_iota(jnp.int32, (tq, tk), 0)
    col = lax.broadcasted_iota(jnp.int32, (tq, tk), 1)
    rel = d * tq + col - row
    half = N_BUCKETS // 2
    max_exact = half // 2
    n = jnp.minimum(jnp.abs(rel), MAX_DISTANCE)
    n2 = n * n
    large = jnp.full((tq, tk), max_exact, jnp.int32)
    for t in range(1, half - max_exact):
        large = large + jnp.where(n2 >= (max_exact * max_exact) * 2 ** t, 1, 0)
    bucket = jnp.where(rel > 0, half, 0) + jnp.where(n < max_exact, n, large)
    val = jnp.full((tq, tk), tab_ref[0, h], F32)
    for bkt in range(1, N_BUCKETS):
        val = jnp.where(bucket == bkt, tab_ref[bkt, h], val)
    o_ref[0, 0] = val * LOG2_E


def _bias_tiles(rel_bias, tq, tk):
    n_heads = rel_bias.shape[1]
    d_lo, d_hi = _bias_offset_range(tq, tk)
    n_off = d_hi - d_lo + 1
    return pl.pallas_call(
        functools.partial(_bias_body, tq=tq, tk=tk, d_lo=d_lo),
        grid=(n_off, n_heads),
        in_specs=[pl.BlockSpec(memory_space=pltpu.SMEM)],
        out_specs=pl.BlockSpec((1, 1, tq, tk), lambda d, h: (d, h, 0, 0)),
        out_shape=jax.ShapeDtypeStruct((n_off, n_heads, tq, tk), F32),
        compiler_params=_params(("parallel", "parallel")),
        name="rel_bias_tiles",
    )(rel_bias.astype(F32))


def _diff_attn_body(dl_ref, g_ref, q_ref, k_ref, v_ref, bias_ref, o_ref,
                    qm_scr, s_a, s_b, pm_a, pm_b, m_scr, l_scr, acc_scr, *, tq, tk):
    qi = pl.program_id(2)
    nk = k_ref.shape[1] // tk
    d_lo, d_hi = _bias_offset_range(tq, tk)

    q = q_ref[0]
    lane = lax.broadcasted_iota(jnp.int32, q.shape, 1)
    zero = jnp.zeros_like(q)
    qm_scr[0] = jnp.where(lane < DIFF_HEAD_DIM, q, zero)
    qm_scr[1] = jnp.where(lane >= DIFF_HEAD_DIM, q, zero)
    m_scr[...] = jnp.full(m_scr.shape, -jnp.inf, F32)
    l_scr[...] = jnp.zeros(l_scr.shape, F32)
    acc_scr[...] = jnp.zeros(acc_scr.shape, F32)

    def scores(j, buf):
        s_ref, pm_ref = buf
        k = k_ref[0, pl.ds(pl.multiple_of(j * tk, tk), tk), :]
        bias = bias_ref[jnp.clip(j * (tk // tq) - qi, d_lo, d_hi) - d_lo, 0]
        for m in range(2):
            s = lax.dot_general(qm_scr[m], k, (((1,), (1,)), ((), ())),
                                preferred_element_type=F32) + bias
            s_ref[m] = s
            pm = s[:, :LANES]
            for c in range(1, tk // LANES):
                pm = jnp.maximum(pm, s[:, c * LANES:(c + 1) * LANES])
            pm_ref[m] = pm

    def softmax_pv(j, buf):
        s_ref, pm_ref = buf
        v = v_ref[0, pl.ds(pl.multiple_of(j * tk, tk), tk), :]
        for m in range(2):
            m_prev = m_scr[m]
            m_next = jnp.maximum(m_prev, jnp.max(pm_ref[m], axis=1, keepdims=True))
            alpha = jnp.exp2(m_prev - m_next)
            acc = alpha * acc_scr[m]
            psum = jnp.zeros_like(alpha)
            m_rep = jnp.concatenate([m_next] * (MXU_DEPTH // LANES), axis=1)
            for c in range(tk // MXU_DEPTH):
                cols = slice(c * MXU_DEPTH, (c + 1) * MXU_DEPTH)
                p = jnp.exp2(s_ref[m, :, cols] - m_rep)
                for i in range(MXU_DEPTH // LANES):
                    psum = psum + p[:, i * LANES:(i + 1) * LANES]
                acc = acc + jnp.dot(p.astype(BF16), v[cols, :], preferred_element_type=F32)
            l_scr[m] = alpha * l_scr[m] + jnp.sum(psum, axis=1, keepdims=True)
            acc_scr[m] = acc
            m_scr[m] = m_next

    buf_a, buf_b = (s_a, pm_a), (s_b, pm_b)
    scores(0, buf_a)

    def pair(jj, carry):
        j = 2 * jj
        scores(j + 1, buf_b)
        softmax_pv(j, buf_a)
        scores(j + 2, buf_a)
        softmax_pv(j + 1, buf_b)
        return carry

    lax.fori_loop(0, nk // 2 - 1, pair, 0)
    scores(nk - 1, buf_b)
    softmax_pv(nk - 2, buf_a)
    softmax_pv(nk - 1, buf_b)

    dl = dl_ref[...]
    lam = (jnp.exp(jnp.sum(dl[0:1] * dl[1:2], axis=1, keepdims=True))
           - jnp.exp(jnp.sum(dl[2:3] * dl[3:4], axis=1, keepdims=True)) + LAM_INIT)
    o = acc_scr[0] / l_scr[0] - lam * (acc_scr[1] / l_scr[1])
    o = o * lax.rsqrt(jnp.mean(o * o, axis=-1, keepdims=True) + SUBLN_EPS)
    o_ref[0] = (o * g_ref[...] * (1.0 - LAM_INIT)).astype(o_ref.dtype)


def _diff_attention(q, k, v, bias_tiles, diff_lambda, subln_g, tq, tk):
    bsz, seq, width = q.shape
    n_heads = width // LANES
    assert seq % tq == 0 and seq % (2 * tk) == 0 and tk % tq == 0 and tk % MXU_DEPTH == 0
    n_off = bias_tiles.shape[0]

    return pl.pallas_call(
        functools.partial(_diff_attn_body, tq=tq, tk=tk),
        grid=(bsz, n_heads, seq // tq),
        in_specs=[
            pl.BlockSpec(diff_lambda.shape, lambda b, h, qi: (0, 0)),
            pl.BlockSpec(subln_g.shape, lambda b, h, qi: (0, 0)),
            pl.BlockSpec((1, tq, LANES), lambda b, h, qi: (b, qi, h)),
            pl.BlockSpec((1, seq, LANES), lambda b, h, qi: (b, 0, h)),
            pl.BlockSpec((1, seq, LANES), lambda b, h, qi: (b, 0, h)),
            pl.BlockSpec((n_off, 1, tq, tk), lambda b, h, qi: (0, h, 0, 0)),
        ],
        out_specs=pl.BlockSpec((1, tq, LANES), lambda b, h, qi: (b, qi, h)),
        out_shape=jax.ShapeDtypeStruct((bsz, seq, width), BF16),
        scratch_shapes=[
            pltpu.VMEM((2, tq, LANES), BF16),
            pltpu.VMEM((2, tq, tk), F32),
            pltpu.VMEM((2, tq, tk), F32),
            pltpu.VMEM((2, tq, LANES), F32),
            pltpu.VMEM((2, tq, LANES), F32),
            pltpu.VMEM((2, tq, LANES), F32),
            pltpu.VMEM((2, tq, LANES), F32),
            pltpu.VMEM((2, tq, LANES), F32),
        ],
        compiler_params=_params(("parallel", "parallel", "arbitrary")),
        name="diff_attention",
    )(diff_lambda, subln_g, q, k, v, bias_tiles)


def _expand_groups(compact, inner, row_div):
    rows, n_cols = compact.shape[1], compact.shape[2]
    src = jnp.arange(n_cols)[:, None]
    dst = jnp.arange(n_cols * GROUPS_PER_SLAB)[None, :]
    same_a = src // inner == dst // (GROUPS_PER_SLAB * inner)
    same_k = src % inner == dst % inner
    expand = (same_a & same_k).astype(BF16)
    dense = jnp.einsum('ork,kc->orc', compact.astype(BF16), expand, preferred_element_type=F32)
    row_group = (jnp.arange(rows)[:, None] // row_div) % GROUPS_PER_SLAB
    col_group = (dst // inner) % GROUPS_PER_SLAB
    return jnp.where((row_group == col_group)[None], dense, 0.0).astype(BF16)


def _ssm_matrices(lam_re, lam_im, log_step, b_re, b_im, c_re, c_im):
    t_len = SSM_CHUNK
    n_groups = lam_re.shape[1]
    n_slabs = n_groups // GROUPS_PER_SLAB
    lam = lax.complex(lam_re.astype(F32), lam_im.astype(F32))
    step = jnp.exp(log_step.astype(F32))[..., None]
    lam_dt = lam * step
    a_bar = jnp.exp(lam_dt)
    b_bar = ((a_bar - 1.0) / lam)[..., None] * lax.complex(b_re.astype(F32), b_im.astype(F32))
    c = lax.complex(c_re.astype(F32), c_im.astype(F32))
    ramp = jnp.arange(t_len + 1, dtype=F32)[None, :, None, None]
    pows = jnp.exp(lam_dt[:, None] * ramp)
    rpows = jnp.exp(lam_dt[:, None] * (t_len - ramp))

    kern = jnp.einsum('dghp,dngp,dgpi->dnghi', c, pows[:, :t_len], b_bar).real
    s_idx = jnp.arange(t_len)[:, None]
    t_idx = jnp.arange(t_len)[None, :]
    fwd = jnp.where((t_idx >= s_idx)[..., None, None, None],
                    kern[0][jnp.clip(t_idx - s_idx, 0, t_len - 1)], 0.0)
    bwd = jnp.where((s_idx >= t_idx)[..., None, None, None],
                    kern[1][jnp.clip(s_idx - t_idx, 0, t_len - 1)], 0.0)
    m_full = (fwd + bwd).transpose(2, 0, 4, 1, 3)
    m_full = m_full.reshape(n_slabs, GROUPS_PER_SLAB, t_len, SSM_GROUP, t_len * SSM_GROUP)
    m_slab = _expand_groups(m_full.transpose(0, 2, 1, 3, 4).reshape(n_slabs, SLAB_K, -1),
                            inner=SSM_GROUP, row_div=SSM_GROUP)

    f_fwd = rpows[0, 1:t_len + 1][:, :, :, None] * b_bar[0][None]
    f_bwd = pows[1, :t_len][:, :, :, None] * b_bar[1][None]
    f_c = jnp.stack([f_fwd, f_bwd])
    f_ri = jnp.stack([f_c.real, f_c.imag], axis=1)
    f_ri = f_ri.reshape(2, 2, t_len, n_slabs, GROUPS_PER_SLAB, SSM_STATE, SSM_GROUP)
    f_slab = _expand_groups(f_ri.transpose(3, 2, 4, 6, 0, 1, 5).reshape(n_slabs, SLAB_K, -1),
                            inner=SSM_STATE, row_div=SSM_GROUP)

    e_fwd = c[0][None] * pows[0, 1:t_len + 1][:, :, None, :]
    e_bwd = c[1][None] * rpows[1, :t_len][:, :, None, :]
    e_c = jnp.stack([e_fwd, e_bwd])
    e_ri = jnp.stack([e_c.real, -e_c.imag], axis=1)
    e_ri = e_ri.reshape(2, 2, t_len, n_slabs, GROUPS_PER_SLAB, SSM_GROUP, SSM_STATE)
    e_slab = _expand_groups(e_ri.transpose(3, 0, 1, 4, 6, 2, 5).reshape(n_slabs, 4 * SLAB_STATE, -1),
                            inner=SSM_GROUP, row_div=SSM_STATE)

    w_out = jnp.concatenate([m_slab, e_slab], axis=1)
    a_chunk = pows[:, t_len].reshape(2, n_slabs, SLAB_STATE).transpose(1, 0, 2)
    return f_slab, w_out, a_chunk.real, a_chunk.imag


def _load_chunks(u_ref, a_scr, nct):
    for s in range(SSM_CHUNK):
        a_scr[:, s * LANES:(s + 1) * LANES] = (
            u_ref[0, pl.ds(s, nct, stride=SSM_CHUNK), :].astype(BF16))


def _ssm_summary_body(u_ref, f_ref, s_ref, a_scr):
    nct = s_ref.shape[1]
    _load_chunks(u_ref, a_scr, nct)
    s_ref[0] = jnp.dot(a_scr[...], f_ref[0], preferred_element_type=F32)


def _ssm_scan_body(ar_ref, ai_ref, s_ref, x_ref):
    nc = s_ref.shape[1]
    w = SLAB_STATE
    ar_f, ai_f = ar_ref[0, 0:1, :], ai_ref[0, 0:1, :]
    ar_b, ai_b = ar_ref[0, 1:2, :], ai_ref[0, 1:2, :]

    def step(c, carry):
        fr, fi, br, bi = carry
        cb = nc - 1 - c
        x_ref[0, pl.ds(c, 1), 0:w] = fr
        x_ref[0, pl.ds(c, 1), w:2 * w] = fi
        x_ref[0, pl.ds(cb, 1), 2 * w:3 * w] = br
        x_ref[0, pl.ds(cb, 1), 3 * w:4 * w] = bi
        sfr = s_ref[0, pl.ds(c, 1), 0:w]
        sfi = s_ref[0, pl.ds(c, 1), w:2 * w]
        sbr = s_ref[0, pl.ds(cb, 1), 2 * w:3 * w]
        sbi = s_ref[0, pl.ds(cb, 1), 3 * w:4 * w]
        return (ar_f * fr - ai_f * fi + sfr, ar_f * fi + ai_f * fr + sfi,
                ar_b * br - ai_b * bi + sbr, ar_b * bi + ai_b * br + sbi)

    zero = jnp.zeros((1, w), F32)
    lax.fori_loop(0, nc, step, (zero, zero, zero, zero))


def _ssm_output_body(u_ref, x_ref, w_ref, y_ref, a_scr):
    nct = x_ref.shape[1]
    _load_chunks(u_ref, a_scr, nct)
    a_scr[:, SLAB_K:] = x_ref[0].astype(BF16)
    y = jnp.dot(a_scr[...], w_ref[0], preferred_element_type=F32)
    for t in range(SSM_CHUNK):
        y_ref[0, pl.ds(t, nct, stride=SSM_CHUNK), :] = y[:, t * LANES:(t + 1) * LANES]


def _s5_scan(u, f_slab, w_out, a_re, a_im, chunks_per_tile=256):
    bsz, seq, width = u.shape
    n_slabs = width // LANES
    nc = seq // SSM_CHUNK
    nct = _row_tile(nc, chunks_per_tile)
    rows = nct * SSM_CHUNK
    state_w = 4 * SLAB_STATE
    grid = (n_slabs, bsz, nc // nct)
    u_spec = pl.BlockSpec((1, rows, LANES), lambda o, b, i: (b, i, o))
    st_spec = pl.BlockSpec((1, nct, state_w), lambda o, b, i: (b, i, o))

    sums = pl.pallas_call(
        _ssm_summary_body, grid=grid,
        in_specs=[u_spec, pl.BlockSpec((1, SLAB_K, state_w), lambda o, b, i: (o, 0, 0))],
        out_specs=st_spec,
        out_shape=jax.ShapeDtypeStruct((bsz, nc, n_slabs * state_w), F32),
        scratch_shapes=[pltpu.VMEM((nct, SLAB_K), BF16)],
        compiler_params=_params(("parallel", "parallel", "parallel")),
        name="s5_chunk_summaries",
    )(u, f_slab)

    carried = pl.pallas_call(
        _ssm_scan_body, grid=(n_slabs, bsz),
        in_specs=[pl.BlockSpec((1, 2, SLAB_STATE), lambda o, b: (o, 0, 0)),
                  pl.BlockSpec((1, 2, SLAB_STATE), lambda o, b: (o, 0, 0)),
                  pl.BlockSpec((1, nc, state_w), lambda o, b: (b, 0, o))],
        out_specs=pl.BlockSpec((1, nc, state_w), lambda o, b: (b, 0, o)),
        out_shape=jax.ShapeDtypeStruct(sums.shape, F32),
        compiler_params=_params(("parallel", "parallel")),
        name="s5_chunk_scan",
    )(a_re, a_im, sums)

    return pl.pallas_call(
        _ssm_output_body, grid=grid,
        in_specs=[u_spec, st_spec,
                  pl.BlockSpec((1, SLAB_K + state_w, SLAB_K), lambda o, b, i: (o, 0, 0),
                               pipeline_mode=pl.Buffered(1))],
        out_specs=u_spec,
        out_shape=jax.ShapeDtypeStruct((bsz, seq, width), F32),
        scratch_shapes=[pltpu.VMEM((nct, SLAB_K + state_w), BF16)],
        compiler_params=_params(("parallel", "parallel", "parallel")),
        name="s5_outputs",
    )(u, carried, w_out)


def _mix_body(x_ref, od_ref, y_ref, u_ref, d_ref, gw_ref, gb_ref, wo_ref, g_ref, b_ref, o_ref):
    x = x_ref[...]
    y = y_ref[...] + d_ref[...] * u_ref[...]
    z = jax.nn.gelu(y)
    gate = jax.nn.sigmoid(jnp.dot(z.astype(BF16), gw_ref[...], preferred_element_type=F32)
                          + gb_ref[...])
    o_ssm = (z * gate).astype(BF16)
    dw = od_ref.shape[1]
    mix = (jnp.dot(od_ref[...], wo_ref[:dw, :], preferred_element_type=F32)
           + jnp.dot(o_ssm, wo_ref[dw:, :], preferred_element_type=F32))
    o_ref[...] = _layer_norm(ALPHA * x + mix, g_ref[...], b_ref[...])


def _mix_ln(x, o_diff, y_ssm, u, d, glu_w, glu_b, w_out, g, b, tm=512):
    n, dm = x.shape
    w = u.shape[1]
    tm = _row_tile(n, tm)
    row = lambda width: pl.BlockSpec((tm, width), lambda i: (i, 0))
    consts = [d, glu_w, glu_b, w_out, g, b]
    return pl.pallas_call(
        _mix_body, grid=(n // tm,),
        in_specs=[row(dm), row(w), row(w), row(w)] + [_const_spec(c.shape) for c in consts],
        out_specs=row(dm),
        out_shape=jax.ShapeDtypeStruct((n, dm), F32),
        compiler_params=_params(("parallel",)),
        name="mix_ln",
    )(x, o_diff, y_ssm, u, *consts)


def _matmul_body(x_ref, w_ref, o_ref):
    o_ref[...] = jnp.dot(x_ref[...].astype(BF16), w_ref[...],
                         preferred_element_type=F32).astype(o_ref.dtype)


def _matmul_rows(x, w, out_dtype, tm=512):
    n, kd = x.shape
    tm = _row_tile(n, tm)
    return pl.pallas_call(
        _matmul_body, grid=(n // tm,),
        in_specs=[pl.BlockSpec((tm, kd), lambda i: (i, 0)), _const_spec(w.shape)],
        out_specs=pl.BlockSpec((tm, w.shape[1]), lambda i: (i, 0)),
        out_shape=jax.ShapeDtypeStruct((n, w.shape[1]), out_dtype),
        compiler_params=_params(("parallel",)),
        name="mem_kv_proj",
    )(x, w)


def _mem_attn_body(x_ref, kv_ref, wq_ref, wo_ref, g_ref, b_ref, o_ref):
    x = x_ref[...]
    dm = x.shape[1]
    hd = dm // MEM_HEADS
    q = (jnp.dot(x.astype(BF16), wq_ref[...], preferred_element_type=F32)
         * (hd ** -0.5)).astype(BF16)
    heads = []
    for h in range(MEM_HEADS):
        sl = slice(h * hd, (h + 1) * hd)
        s = lax.dot_general(q[:, sl], kv_ref[0, :, sl], (((1,), (1,)), ((), ())),
                            preferred_element_type=F32)
        e = jnp.exp(s - jnp.max(s, axis=1, keepdims=True))
        p = e / jnp.sum(e, axis=1, keepdims=True)
        heads.append(jnp.dot(p.astype(BF16), kv_ref[0, :, dm + h * hd:dm + (h + 1) * hd],
                             preferred_element_type=F32))
    o = jnp.concatenate(heads, axis=1).astype(BF16)
    att = jnp.dot(o, wo_ref[...], preferred_element_type=F32)
    o_ref[...] = _layer_norm(ALPHA * x + att, g_ref[...], b_ref[...])


def _mem_attn_ln(x, kv, seq, wq, wo, g, b, tm=512):
    n, dm = x.shape
    tm = _row_tile(seq, tm)
    tiles_per_seq = seq // tm
    consts = [wq, wo, g, b]
    return pl.pallas_call(
        _mem_attn_body, grid=(n // tm,),
        in_specs=[pl.BlockSpec((tm, dm), lambda i: (i, 0)),
                  pl.BlockSpec((1,) + kv.shape[1:], lambda i: (i // tiles_per_seq, 0, 0))]
                 + [_const_spec(c.shape) for c in consts],
        out_specs=pl.BlockSpec((tm, dm), lambda i: (i, 0)),
        out_shape=jax.ShapeDtypeStruct((n, dm), F32),
        compiler_params=_params(("parallel",)),
        name="mem_attn_ln",
    )(x, kv, *consts)


def _attn_tiles(seq):
    for tq, tk in ((512, 512), (256, 256)):
        if seq % tq == 0 and seq % (2 * tk) == 0:
            return tq, tk
    raise ValueError(f"sequence length {seq} is not a multiple of 512")


def _trunk(x, mem, p, bias_tiles, ssm_mats, attn_tile):
    bsz, seq, dm = x.shape
    n = bsz * seq
    f_slab, w_ssm, a_re, a_im = ssm_mats

    x1, q, k, v, u = _ffn_ln(x.reshape(n, dm), p['ffn1_w1'], p['ffn1_w3'], p['ffn1_w2'],
                             p['ln_ffn1_g'], p['ln_ffn1_b'], w_in=p['w_in'])
    w = q.shape[1]
    o_diff = _diff_attention(q.reshape(bsz, seq, w), k.reshape(bsz, seq, w),
                             v.reshape(bsz, seq, w), bias_tiles,
                             p['diff_lambda'], p['diff_subln_g'], *attn_tile)
    y_ssm = _s5_scan(u.reshape(bsz, seq, w), f_slab, w_ssm, a_re, a_im)
    x2 = _mix_ln(x1, o_diff.reshape(n, w), y_ssm.reshape(n, w), u, p['ssm_d'], p['ssm_glu_w'],
                 p['ssm_glu_b'], p['w_out'], p['ln_mix_g'], p['ln_mix_b'])
    kv = _matmul_rows(mem.reshape(-1, dm), p['mem_wkv'], BF16).reshape(bsz, mem.shape[1], -1)
    x3 = _mem_attn_ln(x2, kv, seq, p['mem_wq'], p['mem_wo'], p['ln_mem_g'], p['ln_mem_b'])
    x4 = _ffn_ln(x3, p['ffn2_w1'], p['ffn2_w3'], p['ffn2_w2'], p['ln_ffn2_g'], p['ln_ffn2_b'])
    return x4.reshape(bsz, seq, dm)


def kernel(x_prompt, x_sample, mem_prompt, mem_sample, ffn1_w13, ffn1_w2, ln_ffn1_g, ln_ffn1_b, w_in, diff_lambda, diff_subln_g, rel_bias, ssm_lam_re, ssm_lam_im, ssm_log_step, ssm_b_re, ssm_b_im, ssm_c_re, ssm_c_im, ssm_d, ssm_glu_w, ssm_glu_b, w_out, ln_mix_g, ln_mix_b, mem_wq, mem_wkv, mem_wo, ln_mem_g, ln_mem_b, ffn2_w13, ffn2_w2, ln_ffn2_g, ln_ffn2_b):
    l = 0
    d_ff = ffn1_w2.shape[1]
    p = dict(
        ffn1_w1=ffn1_w13[l, :, :d_ff].astype(BF16), ffn1_w3=ffn1_w13[l, :, d_ff:].astype(BF16),
        ffn1_w2=ffn1_w2[l].astype(BF16), ln_ffn1_g=ln_ffn1_g.astype(F32), ln_ffn1_b=ln_ffn1_b.astype(F32),
        w_in=w_in[l].astype(BF16),
        diff_lambda=diff_lambda[l].astype(F32), diff_subln_g=diff_subln_g.astype(F32),
        ssm_d=ssm_d.astype(F32), ssm_glu_w=ssm_glu_w[l].astype(BF16), ssm_glu_b=ssm_glu_b.astype(F32),
        w_out=w_out[l].astype(BF16), ln_mix_g=ln_mix_g.astype(F32), ln_mix_b=ln_mix_b.astype(F32),
        mem_wq=mem_wq[l].astype(BF16), mem_wkv=mem_wkv[l].astype(BF16), mem_wo=mem_wo[l].astype(BF16),
        ln_mem_g=ln_mem_g.astype(F32), ln_mem_b=ln_mem_b.astype(F32),
        ffn2_w1=ffn2_w13[l, :, :d_ff].astype(BF16), ffn2_w3=ffn2_w13[l, :, d_ff:].astype(BF16),
        ffn2_w2=ffn2_w2[l].astype(BF16), ln_ffn2_g=ln_ffn2_g.astype(F32), ln_ffn2_b=ln_ffn2_b.astype(F32),
    )
    ssm_mats = _ssm_matrices(ssm_lam_re[l], ssm_lam_im[l], ssm_log_step[l], ssm_b_re[l], ssm_b_im[l],
                             ssm_c_re[l], ssm_c_im[l])
    outs = []
    tiles = {}
    for x, mem in ((x_prompt, mem_prompt), (x_sample, mem_sample)):
        t = _attn_tiles(x.shape[1])
        if t not in tiles:
            tiles[t] = _bias_tiles(rel_bias, *t)
        outs.append(_trunk(x, mem, p, tiles[t], ssm_mats, t))
    return tuple(outs)
```

```python
import functools
import math

import jax
import jax.numpy as jnp
from jax import lax
from jax.experimental import pallas as pl
from jax.experimental.pallas import tpu as pltpu

F32 = jnp.float32
BF16 = jnp.bfloat16

DEPTH = 1
ALPHA = (2 * DEPTH) ** 0.25
LN_EPS = 1e-5
SUBLN_EPS = 1e-5
LAM_INIT = 0.8 - 0.6 * math.exp(-0.3 * 0)
LOG2_E = 1.0 / math.log(2.0)

LANES = 128
MXU_DEPTH = 256
DIFF_HEADS = 4
DIFF_HEAD_DIM = 64
N_BUCKETS = 32
MEM_HEADS = 4
SSM_GROUP = 16
SSM_STATE = 64
SSM_CHUNK = 16
GROUPS_PER_SLAB = LANES // SSM_GROUP
SLAB_K = SSM_CHUNK * LANES
SLAB_STATE = GROUPS_PER_SLAB * SSM_STATE

VMEM_LIMIT_BYTES = 56 * 1024 * 1024


def _params(semantics, flags=None):
    return pltpu.CompilerParams(dimension_semantics=semantics,
                                vmem_limit_bytes=VMEM_LIMIT_BYTES, flags=flags)


def _const_spec(shape):
    nd = len(shape)
    return pl.BlockSpec(shape, lambda *_: (0,) * nd, pipeline_mode=pl.Buffered(1))


def _layer_norm(y, g, b):
    mu = jnp.mean(y, axis=-1, keepdims=True)
    yc = y - mu
    var = jnp.mean(yc * yc, axis=-1, keepdims=True)
    return yc * lax.rsqrt(var + LN_EPS) * g + b


def _row_tile(n_rows, want):
    t = min(want, n_rows)
    assert n_rows % t == 0
    return t


def _ffn_body(x_ref, w1_ref, w3_ref, w2_ref, g_ref, b_ref, *rest, f_chunk, with_proj):
    x = x_ref[...]
    xb = x.astype(BF16)
    d_ff = w1_ref.shape[1]
    acc = jnp.zeros(x.shape, F32)
    for c in range(d_ff // f_chunk):
        sl = slice(c * f_chunk, (c + 1) * f_chunk)
        a = jnp.dot(xb, w1_ref[:, sl], preferred_element_type=F32)
        b = jnp.dot(xb, w3_ref[:, sl], preferred_element_type=F32)
        gated = (a * jax.nn.sigmoid(a) * b).astype(BF16)
        acc = acc + jnp.dot(gated, w2_ref[sl, :], preferred_element_type=F32)
    y = _layer_norm(ALPHA * x + 0.5 * acc, g_ref[...], b_ref[...])
    if not with_proj:
        (o_ref,) = rest
        o_ref[...] = y
        return
    win_ref, o_ref, q_ref, k_ref, v_ref, u_ref = rest
    o_ref[...] = y
    h = jnp.dot(y.astype(BF16), win_ref[...], preferred_element_type=F32)
    w = q_ref.shape[1]
    q_ref[...] = (h[:, :w] * (DIFF_HEAD_DIM ** -0.5 * LOG2_E)).astype(BF16)
    k_ref[...] = h[:, w:2 * w].astype(BF16)
    v_ref[...] = h[:, 2 * w:3 * w].astype(BF16)
    u_ref[...] = h[:, 3 * w:]


def _ffn_ln(x, w1, w3, w2, g, b, w_in=None, tm=512):
    n, d = x.shape
    d_ff = w1.shape[1]
    tm = _row_tile(n, tm)
    f_chunk = d_ff // 2 if (d_ff // 2) % LANES == 0 else d_ff
    row = lambda width: pl.BlockSpec((tm, width), lambda i: (i, 0))
    in_specs = [row(d), _const_spec(w1.shape), _const_spec(w3.shape), _const_spec(w2.shape),
                _const_spec(g.shape), _const_spec(b.shape)]
    args = [x, w1, w3, w2, g, b]
    out_shape = [jax.ShapeDtypeStruct((n, d), F32)]
    out_specs = [row(d)]
    if w_in is not None:
        w = w_in.shape[1] // 4
        in_specs.append(_const_spec(w_in.shape))
        args.append(w_in)
        out_shape += [jax.ShapeDtypeStruct((n, w), BF16)] * 3 + [jax.ShapeDtypeStruct((n, w), F32)]
        out_specs += [row(w)] * 4
    out = pl.pallas_call(
        functools.partial(_ffn_body, f_chunk=f_chunk, with_proj=w_in is not None),
        grid=(n // tm,),
        in_specs=in_specs, out_specs=out_specs, out_shape=out_shape,
        compiler_params=_params(("parallel",)),
        name="ffn_ln_proj" if w_in is not None else "ffn_ln",
    )(*args)
    return out if w_in is not None else out[0]


MAX_DISTANCE = 128


def _bias_offset_range(tq, tk):
    d_lo = -((tk + MAX_DISTANCE - 1 + tq - 1) // tq)
    d_hi = 1 + (MAX_DISTANCE - 1 + tq - 1) // tq
    return d_lo, d_hi


def _bias_body(tab_ref, o_ref, *, tq, tk, d_lo):
    d = pl.program_id(0) + d_lo
    h = pl.program_id(1)
    row = lax.broadcasted_iota(jnp.int32, (tq, tk), 0)
    col = lax.broadcasted_iota(jnp.int32, (tq, tk), 1)
    rel = d * tq + col - row
    half = N_BUCKETS // 2
    max_exact = half // 2
    n = jnp.minimum(jnp.abs(rel), MAX_DISTANCE)
    n2 = n * n
    large = jnp.full((tq, tk), max_exact, jnp.int32)
    for t in range(1, half - max_exact):
        large = large + jnp.where(n2 >= (max_exact * max_exact) * 2 ** t, 1, 0)
    bucket = jnp.where(rel > 0, half, 0) + jnp.where(n < max_exact, n, large)
    val = jnp.full((tq, tk), tab_ref[0, h], F32)
    for bkt in range(1, N_BUCKETS):
        val = jnp.where(bucket == bkt, tab_ref[bkt, h], val)
    o_ref[0, 0] = val * LOG2_E


def _bias_tiles(rel_bias, tq, tk):
    n_heads = rel_bias.shape[1]
    d_lo, d_hi = _bias_offset_range(tq, tk)
    n_off = d_hi - d_lo + 1
    return pl.pallas_call(
        functools.partial(_bias_body, tq=tq, tk=tk, d_lo=d_lo),
        grid=(n_off, n_heads),
        in_specs=[pl.BlockSpec(memory_space=pltpu.SMEM)],
        out_specs=pl.BlockSpec((1, 1, tq, tk), lambda d, h: (d, h, 0, 0)),
        out_shape=jax.ShapeDtypeStruct((n_off, n_heads, tq, tk), F32),
        compiler_params=_params(("parallel", "parallel")),
        name="rel_bias_tiles",
    )(rel_bias.astype(F32))


def _diff_attn_body(dl_ref, g_ref, q_ref, k_ref, v_ref, bias_ref, o_ref,
                    qm_scr, s_a, s_b, pm_a, pm_b, m_scr, l_scr, acc_scr, *, tq, tk):
    qi = pl.program_id(2)
    nk = k_ref.shape[1] // tk
    d_lo, d_hi = _bias_offset_range(tq, tk)
    pv_chunk = min(MXU_DEPTH, tk)

    q = q_ref[0]
    lane = lax.broadcasted_iota(jnp.int32, q.shape, 1)
    zero = jnp.zeros_like(q)
    qm_scr[0] = jnp.where(lane < DIFF_HEAD_DIM, q, zero)
    qm_scr[1] = jnp.where(lane >= DIFF_HEAD_DIM, q, zero)
    m_scr[...] = jnp.full(m_scr.shape, -jnp.inf, F32)
    l_scr[...] = jnp.zeros(l_scr.shape, F32)
    acc_scr[...] = jnp.zeros(acc_scr.shape, F32)

    def scores(j, buf):
        s_ref, pm_ref = buf
        k = k_ref[0, pl.ds(pl.multiple_of(j * tk, tk), tk), :]
        bias = bias_ref[jnp.clip(j * (tk // tq) - qi, d_lo, d_hi) - d_lo, 0]
        for m in range(2):
            s = lax.dot_general(qm_scr[m], k, (((1,), (1,)), ((), ())),
                                preferred_element_type=F32) + bias
            s_ref[m] = s
            pm = s[:, :LANES]
            for c in range(1, tk // LANES):
                pm = jnp.maximum(pm, s[:, c * LANES:(c + 1) * LANES])
            pm_ref[m] = pm

    def softmax_pv(j, buf):
        s_ref, pm_ref = buf
        v = v_ref[0, pl.ds(pl.multiple_of(j * tk, tk), tk), :]
        for m in range(2):
            m_prev = m_scr[m]
            m_next = jnp.maximum(m_prev, jnp.max(pm_ref[m], axis=1, keepdims=True))
            alpha = jnp.exp2(m_prev - m_next)
            acc = alpha * acc_scr[m]
            psum = jnp.zeros_like(alpha)
            m_rep = jnp.concatenate([m_next] * (pv_chunk // LANES), axis=1)
            for c in range(tk // pv_chunk):
                cols = slice(c * pv_chunk, (c + 1) * pv_chunk)
                p = jnp.exp2(s_ref[m, :, cols] - m_rep)
                for i in range(pv_chunk // LANES):
                    psum = psum + p[:, i * LANES:(i + 1) * LANES]
                acc = acc + jnp.dot(p.astype(BF16), v[cols, :], preferred_element_type=F32)
            l_scr[m] = alpha * l_scr[m] + jnp.sum(psum, axis=1, keepdims=True)
            acc_scr[m] = acc
            m_scr[m] = m_next

    buf_a, buf_b = (s_a, pm_a), (s_b, pm_b)
    scores(0, buf_a)

    def pair(jj, carry):
        j = 2 * jj
        scores(j + 1, buf_b)
        softmax_pv(j, buf_a)
        scores(j + 2, buf_a)
        softmax_pv(j + 1, buf_b)
        return carry

    lax.fori_loop(0, nk // 2 - 1, pair, 0)
    scores(nk - 1, buf_b)
    softmax_pv(nk - 2, buf_a)
    softmax_pv(nk - 1, buf_b)

    dl = dl_ref[...]
    lam = (jnp.exp(jnp.sum(dl[0:1] * dl[1:2], axis=1, keepdims=True))
           - jnp.exp(jnp.sum(dl[2:3] * dl[3:4], axis=1, keepdims=True)) + LAM_INIT)
    o = acc_scr[0] / l_scr[0] - lam * (acc_scr[1] / l_scr[1])
    o = o * lax.rsqrt(jnp.mean(o * o, axis=-1, keepdims=True) + SUBLN_EPS)
    o_ref[0] = (o * g_ref[...] * (1.0 - LAM_INIT)).astype(o_ref.dtype)


def _diff_attention(q, k, v, bias_tiles, diff_lambda, subln_g, tq, tk):
    bsz, seq, width = q.shape
    n_heads = width // LANES
    assert seq % tq == 0 and seq % (2 * tk) == 0 and tk % tq == 0 and tk % LANES == 0
    n_off = bias_tiles.shape[0]
    stat = pltpu.VMEM((2, tq, LANES), F32)

    return pl.pallas_call(
        functools.partial(_diff_attn_body, tq=tq, tk=tk),
        grid=(bsz, n_heads, seq // tq),
        in_specs=[
            pl.BlockSpec(diff_lambda.shape, lambda b, h, qi: (0, 0)),
            pl.BlockSpec(subln_g.shape, lambda b, h, qi: (0, 0)),
            pl.BlockSpec((1, tq, LANES), lambda b, h, qi: (b, qi, h)),
            pl.BlockSpec((1, seq, LANES), lambda b, h, qi: (b, 0, h)),
            pl.BlockSpec((1, seq, LANES), lambda b, h, qi: (b, 0, h)),
            pl.BlockSpec((n_off, 1, tq, tk), lambda b, h, qi: (0, h, 0, 0)),
        ],
        out_specs=pl.BlockSpec((1, tq, LANES), lambda b, h, qi: (b, qi, h)),
        out_shape=jax.ShapeDtypeStruct((bsz, seq, width), BF16),
        scratch_shapes=[
            pltpu.VMEM((2, tq, LANES), BF16),
            pltpu.VMEM((2, tq, tk), F32), pltpu.VMEM((2, tq, tk), F32),
            stat, stat,
            stat, stat, stat,
        ],
        compiler_params=_params(("parallel", "parallel", "arbitrary")),
        name="diff_attention",
    )(diff_lambda, subln_g, q, k, v, bias_tiles)


def _expand_groups(compact, inner, row_div):
    rows, n_cols = compact.shape[1], compact.shape[2]
    src = jnp.arange(n_cols)[:, None]
    dst = jnp.arange(n_cols * GROUPS_PER_SLAB)[None, :]
    same_a = src // inner == dst // (GROUPS_PER_SLAB * inner)
    same_k = src % inner == dst % inner
    expand = (same_a & same_k).astype(BF16)
    dense = jnp.einsum('ork,kc->orc', compact.astype(BF16), expand, preferred_element_type=F32)
    row_group = (jnp.arange(rows)[:, None] // row_div) % GROUPS_PER_SLAB
    col_group = (dst // inner) % GROUPS_PER_SLAB
    return jnp.where((row_group == col_group)[None], dense, 0.0).astype(BF16)


def _ssm_matrices(lam_re, lam_im, log_step, b_re, b_im, c_re, c_im):
    t_len = SSM_CHUNK
    n_groups = lam_re.shape[1]
    n_slabs = n_groups // GROUPS_PER_SLAB
    lam = lax.complex(lam_re.astype(F32), lam_im.astype(F32))
    step = jnp.exp(log_step.astype(F32))[..., None]
    lam_dt = lam * step
    a_bar = jnp.exp(lam_dt)
    b_bar = ((a_bar - 1.0) / lam)[..., None] * lax.complex(b_re.astype(F32), b_im.astype(F32))
    c = lax.complex(c_re.astype(F32), c_im.astype(F32))
    ramp = jnp.arange(t_len + 1, dtype=F32)[None, :, None, None]
    pows = jnp.exp(lam_dt[:, None] * ramp)
    rpows = jnp.exp(lam_dt[:, None] * (t_len - ramp))

    kern = jnp.einsum('dghp,dngp,dgpi->dnghi', c, pows[:, :t_len], b_bar).real
    s_idx = jnp.arange(t_len)[:, None]
    t_idx = jnp.arange(t_len)[None, :]
    fwd = jnp.where((t_idx >= s_idx)[..., None, None, None],
                    kern[0][jnp.clip(t_idx - s_idx, 0, t_len - 1)], 0.0)
    bwd = jnp.where((s_idx >= t_idx)[..., None, None, None],
                    kern[1][jnp.clip(s_idx - t_idx, 0, t_len - 1)], 0.0)
    m_full = (fwd + bwd).transpose(2, 0, 4, 1, 3)
    m_full = m_full.reshape(n_slabs, GROUPS_PER_SLAB, t_len, SSM_GROUP, t_len * SSM_GROUP)
    m_slab = _expand_groups(m_full.transpose(0, 2, 1, 3, 4).reshape(n_slabs, SLAB_K, -1),
                            inner=SSM_GROUP, row_div=SSM_GROUP)

    f_fwd = rpows[0, 1:t_len + 1][:, :, :, None] * b_bar[0][None]
    f_bwd = pows[1, :t_len][:, :, :, None] * b_bar[1][None]
    f_c = jnp.stack([f_fwd, f_bwd])
    f_ri = jnp.stack([f_c.real, f_c.imag], axis=1)
    f_ri = f_ri.reshape(2, 2, t_len, n_slabs, GROUPS_PER_SLAB, SSM_STATE, SSM_GROUP)
    f_slab = _expand_groups(f_ri.transpose(3, 2, 4, 6, 0, 1, 5).reshape(n_slabs, SLAB_K, -1),
                            inner=SSM_STATE, row_div=SSM_GROUP)

    e_fwd = c[0][None] * pows[0, 1:t_len + 1][:, :, None, :]
    e_bwd = c[1][None] * rpows[1, :t_len][:, :, None, :]
    e_c = jnp.stack([e_fwd, e_bwd])
    e_ri = jnp.stack([e_c.real, -e_c.imag], axis=1)
    e_ri = e_ri.reshape(2, 2, t_len, n_slabs, GROUPS_PER_SLAB, SSM_GROUP, SSM_STATE)
    e_slab = _expand_groups(e_ri.transpose(3, 0, 1, 4, 6, 2, 5).reshape(n_slabs, 4 * SLAB_STATE, -1),
                            inner=SSM_GROUP, row_div=SSM_STATE)

    w_out = jnp.concatenate([m_slab, e_slab], axis=1)
    a_chunk = pows[:, t_len].reshape(2, n_slabs, SLAB_STATE).transpose(1, 0, 2)
    return f_slab, w_out, a_chunk.real, a_chunk.imag


def _load_chunks(u_ref, a_scr, nct):
    for s in range(SSM_CHUNK):
        a_scr[:, s * LANES:(s + 1) * LANES] = (
            u_ref[0, pl.ds(s, nct, stride=SSM_CHUNK), :].astype(BF16))


def _ssm_summary_body(u_ref, f_ref, s_ref, a_scr):
    nct = s_ref.shape[1]
    _load_chunks(u_ref, a_scr, nct)
    s_ref[0] = jnp.dot(a_scr[...], f_ref[0], preferred_element_type=F32)


def _ssm_scan_body(ar_ref, ai_ref, s_ref, x_ref):
    nc = s_ref.shape[1]
    w = SLAB_STATE
    ar_f, ai_f = ar_ref[0, 0:1, :], ai_ref[0, 0:1, :]
    ar_b, ai_b = ar_ref[0, 1:2, :], ai_ref[0, 1:2, :]

    def step(c, carry):
        fr, fi, br, bi = carry
        cb = nc - 1 - c
        x_ref[0, pl.ds(c, 1), 0:w] = fr
        x_ref[0, pl.ds(c, 1), w:2 * w] = fi
        x_ref[0, pl.ds(cb, 1), 2 * w:3 * w] = br
        x_ref[0, pl.ds(cb, 1), 3 * w:4 * w] = bi
        sfr = s_ref[0, pl.ds(c, 1), 0:w]
        sfi = s_ref[0, pl.ds(c, 1), w:2 * w]
        sbr = s_ref[0, pl.ds(cb, 1), 2 * w:3 * w]
        sbi = s_ref[0, pl.ds(cb, 1), 3 * w:4 * w]
        return (ar_f * fr - ai_f * fi + sfr, ar_f * fi + ai_f * fr + sfi,
                ar_b * br - ai_b * bi + sbr, ar_b * bi + ai_b * br + sbi)

    zero = jnp.zeros((1, w), F32)
    lax.fori_loop(0, nc, step, (zero, zero, zero, zero))


def _ssm_output_body(u_ref, x_ref, w_ref, y_ref, a_scr):
    nct = x_ref.shape[1]
    _load_chunks(u_ref, a_scr, nct)
    a_scr[:, SLAB_K:] = x_ref[0].astype(BF16)
    y = jnp.dot(a_scr[...], w_ref[0], preferred_element_type=F32)
    for t in range(SSM_CHUNK):
        y_ref[0, pl.ds(t, nct, stride=SSM_CHUNK), :] = y[:, t * LANES:(t + 1) * LANES]


def _s5_scan(u, f_slab, w_out, a_re, a_im, chunks_per_tile=256):
    bsz, seq, width = u.shape
    n_slabs = width // LANES
    nc = seq // SSM_CHUNK
    nct = _row_tile(nc, chunks_per_tile)
    rows = nct * SSM_CHUNK
    state_w = 4 * SLAB_STATE
    grid = (n_slabs, bsz, nc // nct)
    u_spec = pl.BlockSpec((1, rows, LANES), lambda o, b, i: (b, i, o))
    st_spec = pl.BlockSpec((1, nct, state_w), lambda o, b, i: (b, i, o))

    sums = pl.pallas_call(
        _ssm_summary_body, grid=grid,
        in_specs=[u_spec, pl.BlockSpec((1, SLAB_K, state_w), lambda o, b, i: (o, 0, 0))],
        out_specs=st_spec,
        out_shape=jax.ShapeDtypeStruct((bsz, nc, n_slabs * state_w), F32),
        scratch_shapes=[pltpu.VMEM((nct, SLAB_K), BF16)],
        compiler_params=_params(("parallel", "parallel", "parallel")),
        name="s5_chunk_summaries",
    )(u, f_slab)

    carried = pl.pallas_call(
        _ssm_scan_body, grid=(n_slabs, bsz),
        in_specs=[pl.BlockSpec((1, 2, SLAB_STATE), lambda o, b: (o, 0, 0)),
                  pl.BlockSpec((1, 2, SLAB_STATE), lambda o, b: (o, 0, 0)),
                  pl.BlockSpec((1, nc, state_w), lambda o, b: (b, 0, o))],
        out_specs=pl.BlockSpec((1, nc, state_w), lambda o, b: (b, 0, o)),
        out_shape=jax.ShapeDtypeStruct(sums.shape, F32),
        compiler_params=_params(("parallel", "parallel")),
        name="s5_chunk_scan",
    )(a_re, a_im, sums)

    return pl.pallas_call(
        _ssm_output_body, grid=grid,
        in_specs=[u_spec, st_spec,
                  pl.BlockSpec((1, SLAB_K + state_w, SLAB_K), lambda o, b, i: (o, 0, 0),
                               pipeline_mode=pl.Buffered(1))],
        out_specs=u_spec,
        out_shape=jax.ShapeDtypeStruct((bsz, seq, width), F32),
        scratch_shapes=[pltpu.VMEM((nct, SLAB_K + state_w), BF16)],
        compiler_params=_params(("parallel", "parallel", "parallel")),
        name="s5_outputs",
    )(u, carried, w_out)


def _mix_body(x_ref, od_ref, y_ref, u_ref, d_ref, gw_ref, gb_ref, wo_ref, g_ref, b_ref, o_ref,
              *, sub_rows):
    dw = od_ref.shape[1]
    for r in range(x_ref.shape[0] // sub_rows):
        rows = slice(r * sub_rows, (r + 1) * sub_rows)
        y = y_ref[rows, :] + d_ref[...] * u_ref[rows, :]
        z = jax.nn.gelu(y)
        gate = jax.nn.sigmoid(jnp.dot(z.astype(BF16), gw_ref[...], preferred_element_type=F32)
                              + gb_ref[...])
        o_ssm = (z * gate).astype(BF16)
        mix = (jnp.dot(od_ref[rows, :], wo_ref[:dw, :], preferred_element_type=F32)
               + jnp.dot(o_ssm, wo_ref[dw:, :], preferred_element_type=F32))
        o_ref[rows, :] = _layer_norm(ALPHA * x_ref[rows, :] + mix, g_ref[...], b_ref[...])


def _mix_ln(x, o_diff, y_ssm, u, d, glu_w, glu_b, w_out, g, b, tm=1024, sub_rows=512):
    n, dm = x.shape
    w = u.shape[1]
    tm = _row_tile(n, tm)
    sub_rows = min(sub_rows, tm)
    row = lambda width: pl.BlockSpec((tm, width), lambda i: (i, 0))
    consts = [d, glu_w, glu_b, w_out, g, b]
    return pl.pallas_call(
        functools.partial(_mix_body, sub_rows=sub_rows), grid=(n // tm,),
        in_specs=[row(dm), row(w), row(w), row(w)] + [_const_spec(c.shape) for c in consts],
        out_specs=row(dm),
        out_shape=jax.ShapeDtypeStruct((n, dm), F32),
        compiler_params=_params(("parallel",)),
        name="mix_ln",
    )(x, o_diff, y_ssm, u, *consts)


def _matmul_body(x_ref, w_ref, o_ref):
    o_ref[...] = jnp.dot(x_ref[...].astype(BF16), w_ref[...],
                         preferred_element_type=F32).astype(o_ref.dtype)


def _matmul_rows(x, w, out_dtype, tm=512):
    n, kd = x.shape
    tm = _row_tile(n, tm)
    return pl.pallas_call(
        _matmul_body, grid=(n // tm,),
        in_specs=[pl.BlockSpec((tm, kd), lambda i: (i, 0)), _const_spec(w.shape)],
        out_specs=pl.BlockSpec((tm, w.shape[1]), lambda i: (i, 0)),
        out_shape=jax.ShapeDtypeStruct((n, w.shape[1]), out_dtype),
        compiler_params=_params(("parallel",)),
        name="mem_kv_proj",
    )(x, w)


def _mem_attn_body(x_ref, kv_ref, wq_ref, wo_ref, g_ref, b_ref, o_ref, *, sub_rows):
    for r in range(x_ref.shape[0] // sub_rows):
        rows = slice(r * sub_rows, (r + 1) * sub_rows)
        o_ref[rows, :] = _mem_attn_rows(x_ref[rows, :], kv_ref, wq_ref, wo_ref, g_ref, b_ref)


def _mem_attn_rows(x, kv_ref, wq_ref, wo_ref, g_ref, b_ref):
    dm = x.shape[1]
    hd = dm // MEM_HEADS
    q = (jnp.dot(x.astype(BF16), wq_ref[...], preferred_element_type=F32)
         * (hd ** -0.5)).astype(BF16)
    heads = []
    for h in range(MEM_HEADS):
        sl = slice(h * hd, (h + 1) * hd)
        s = lax.dot_general(q[:, sl], kv_ref[0, :, sl], (((1,), (1,)), ((), ())),
                            preferred_element_type=F32)
        e = jnp.exp(s - jnp.max(s, axis=1, keepdims=True))
        p = e / jnp.sum(e, axis=1, keepdims=True)
        heads.append(jnp.dot(p.astype(BF16), kv_ref[0, :, dm + h * hd:dm + (h + 1) * hd],
                             preferred_element_type=F32))
    o = jnp.concatenate(heads, axis=1).astype(BF16)
    att = jnp.dot(o, wo_ref[...], preferred_element_type=F32)
    return _layer_norm(ALPHA * x + att, g_ref[...], b_ref[...])


def _mem_attn_ln(x, kv, seq, wq, wo, g, b, tm=1024, sub_rows=512):
    n, dm = x.shape
    tm = _row_tile(seq, tm)
    sub_rows = min(sub_rows, tm)
    tiles_per_seq = seq // tm
    consts = [wq, wo, g, b]
    return pl.pallas_call(
        functools.partial(_mem_attn_body, sub_rows=sub_rows), grid=(n // tm,),
        in_specs=[pl.BlockSpec((tm, dm), lambda i: (i, 0)),
                  pl.BlockSpec((1,) + kv.shape[1:], lambda i: (i // tiles_per_seq, 0, 0))]
                 + [_const_spec(c.shape) for c in consts],
        out_specs=pl.BlockSpec((tm, dm), lambda i: (i, 0)),
        out_shape=jax.ShapeDtypeStruct((n, dm), F32),
        compiler_params=_params(("parallel",)),
        name="mem_attn_ln",
    )(x, kv, *consts)


def _attn_tiles(seq):
    for tq, tk in ((512, 1024), (512, 512), (256, 256), (128, 128)):
        if seq % tq == 0 and seq % (2 * tk) == 0:
            return tq, tk
    raise ValueError(f"sequence length {seq} is not a multiple of 512")


def _trunk(x, mem, p, bias_tiles, ssm_mats, attn_tile):
    bsz, seq, dm = x.shape
    n = bsz * seq
    f_slab, w_ssm, a_re, a_im = ssm_mats

    x1, q, k, v, u = _ffn_ln(x.reshape(n, dm), p['ffn1_w1'], p['ffn1_w3'], p['ffn1_w2'],
                             p['ln_ffn1_g'], p['ln_ffn1_b'], w_in=p['w_in'])
    w = q.shape[1]
    o_diff = _diff_attention(q.reshape(bsz, seq, w), k.reshape(bsz, seq, w),
                             v.reshape(bsz, seq, w), bias_tiles,
                             p['diff_lambda'], p['diff_subln_g'], *attn_tile)
    y_ssm = _s5_scan(u.reshape(bsz, seq, w), f_slab, w_ssm, a_re, a_im)
    x2 = _mix_ln(x1, o_diff.reshape(n, w), y_ssm.reshape(n, w), u, p['ssm_d'], p['ssm_glu_w'],
                 p['ssm_glu_b'], p['w_out'], p['ln_mix_g'], p['ln_mix_b'])
    kv = _matmul_rows(mem.reshape(-1, dm), p['mem_wkv'], BF16).reshape(bsz, mem.shape[1], -1)
    x3 = _mem_attn_ln(x2, kv, seq, p['mem_wq'], p['mem_wo'], p['ln_mem_g'], p['ln_mem_b'])
    x4 = _ffn_ln(x3, p['ffn2_w1'], p['ffn2_w3'], p['ffn2_w2'], p['ln_ffn2_g'], p['ln_ffn2_b'])
    return x4.reshape(bsz, seq, dm)


def kernel(x_prompt, x_sample, mem_prompt, mem_sample, ffn1_w13, ffn1_w2, ln_ffn1_g, ln_ffn1_b, w_in, diff_lambda, diff_subln_g, rel_bias, ssm_lam_re, ssm_lam_im, ssm_log_step, ssm_b_re, ssm_b_im, ssm_c_re, ssm_c_im, ssm_d, ssm_glu_w, ssm_glu_b, w_out, ln_mix_g, ln_mix_b, mem_wq, mem_wkv, mem_wo, ln_mem_g, ln_mem_b, ffn2_w13, ffn2_w2, ln_ffn2_g, ln_ffn2_b):
    l = 0
    d_ff = ffn1_w2.shape[1]
    p = dict(
        ffn1_w1=ffn1_w13[l, :, :d_ff].astype(BF16), ffn1_w3=ffn1_w13[l, :, d_ff:].astype(BF16),
        ffn1_w2=ffn1_w2[l].astype(BF16), ln_ffn1_g=ln_ffn1_g.astype(F32), ln_ffn1_b=ln_ffn1_b.astype(F32),
        w_in=w_in[l].astype(BF16),
        diff_lambda=diff_lambda[l].astype(F32), diff_subln_g=diff_subln_g.astype(F32),
        ssm_d=ssm_d.astype(F32), ssm_glu_w=ssm_glu_w[l].astype(BF16), ssm_glu_b=ssm_glu_b.astype(F32),
        w_out=w_out[l].astype(BF16), ln_mix_g=ln_mix_g.astype(F32), ln_mix_b=ln_mix_b.astype(F32),
        mem_wq=mem_wq[l].astype(BF16), mem_wkv=mem_wkv[l].astype(BF16), mem_wo=mem_wo[l].astype(BF16),
        ln_mem_g=ln_mem_g.astype(F32), ln_mem_b=ln_mem_b.astype(F32),
        ffn2_w1=ffn2_w13[l, :, :d_ff].astype(BF16), ffn2_w3=ffn2_w13[l, :, d_ff:].astype(BF16),
        ffn2_w2=ffn2_w2[l].astype(BF16), ln_ffn2_g=ln_ffn2_g.astype(F32), ln_ffn2_b=ln_ffn2_b.astype(F32),
    )
    ssm_mats = _ssm_matrices(ssm_lam_re[l], ssm_lam_im[l], ssm_log_step[l], ssm_b_re[l], ssm_b_im[l],
                             ssm_c_re[l], ssm_c_im[l])
    outs = []
    tiles = {}
    for x, mem in ((x_prompt, mem_prompt), (x_sample, mem_sample)):
        t = _attn_tiles(x.shape[1])
        if t not in tiles:
            tiles[t] = _bias_tiles(rel_bias, *t)
        outs.append(_trunk(x, mem, p, tiles[t], ssm_mats, t))
    return tuple(outs)
```
